```python
import numpy as np
import jax
import jax.numpy as jnp
from jax import lax


D_MODEL = 1024
BATCH = 16
SEQ = 2048
DEPTH = 2

N_BRANCH = 4
BRANCH_WIDTH = 512
CONV_TAPS = 3
SC_WIDTH = BRANCH_WIDTH
NSA_HEADS = 8
NSA_KV_GROUPS = 2
NSA_HPG = NSA_HEADS // NSA_KV_GROUPS
NSA_HEAD_DIM = BRANCH_WIDTH // NSA_HEADS
CMP_BLOCK = 32
CMP_STRIDE = 16
SEL_BLOCK = 64
SEL_TOP_N = 16
SEL_Q_CHUNK = 32
WINDOW = 512
WIN_Q_BLOCK = 128
ROPE_THETA = 500000.0
ROPE_DIM = NSA_HEAD_DIM // 4
FORCE_SCORE = 1.0e4
RET_HEADS = 4
RET_HEAD_DIM = BRANCH_WIDTH // RET_HEADS
RET_CHUNK = 128
RET_THETA = 10000.0
POOL_WINDOWS = (2, 4, 8, 16)
POOL_GROUPS = 4
POOL_GROUP_DIM = BRANCH_WIDTH // POOL_GROUPS
D_FF = 2816
EPS = 1e-6
NEG_INF = -1e30

IN_SPLITS = (3 * SC_WIDTH, NSA_HEADS * NSA_HEAD_DIM, 3 * 2 * NSA_KV_GROUPS * NSA_HEAD_DIM, 3 * NSA_HEADS, 4 * BRANCH_WIDTH, BRANCH_WIDTH, N_BRANCH * D_MODEL)
D_IN = sum(IN_SPLITS)

kernel_name = 'hybrid_gated_parallel_mixer'


def rms_norm(x, g):
    x32 = x.astype(jnp.float32)
    y = x32 * lax.rsqrt(jnp.mean(x32 * x32, axis=-1, keepdims=True) + EPS)
    return (y * g.astype(jnp.float32)).astype(x.dtype)


def rotary(x, pos, rot_dim, theta):
    half = rot_dim // 2
    inv_freq = np.power(np.float32(theta), -np.arange(half, dtype=np.float32) / np.float32(half))
    ang = pos.astype(jnp.float32)[:, None] * jnp.asarray(inv_freq)[None, :]
    cos = jnp.cos(ang)[:, None, :].astype(x.dtype)
    sin = jnp.sin(ang)[:, None, :].astype(x.dtype)
    x1 = x[..., :half]
    x2 = x[..., half:rot_dim]
    return jnp.concatenate([x1 * cos - x2 * sin, x2 * cos + x1 * sin, x[..., rot_dim:]], axis=-1)


def causal_dwconv(x, w):
    k_taps, ch = w.shape
    return lax.conv_general_dilated(x, w[:, None, :], window_strides=(1,), padding=((k_taps - 1, 0),), dimension_numbers=('NWC', 'WIO', 'NWC'), feature_group_count=ch)


def masked_softmax(s, mask):
    return jax.nn.softmax(jnp.where(mask, s.astype(jnp.float32), NEG_INF), axis=-1)


def short_conv_mixer(z, conv_w):
    b, c, h = jnp.split(z, 3, axis=-1)
    return b * causal_dwconv(c * h, conv_w)


def nsa_mixer(zq, zkv, zg, cmp_pe, cmp_w1_k, cmp_w2_k, cmp_w1_v, cmp_w2_v):
    B, S, _ = zq.shape
    G, J, Dh = NSA_KV_GROUPS, NSA_HPG, NSA_HEAD_DIM
    pos = jnp.arange(S)
    q = rotary(zq.reshape(B, S, NSA_HEADS, Dh), pos, ROPE_DIM, ROPE_THETA) * (Dh ** -0.5)
    q = q.reshape(B, S, G, J, Dh)
    kv = zkv.reshape(B, S, 3, 2, G, Dh)
    k_cmp, v_cmp = kv[:, :, 0, 0], kv[:, :, 0, 1]
    k_slc, v_slc = rotary(kv[:, :, 1, 0], pos, ROPE_DIM, ROPE_THETA), kv[:, :, 1, 1]
    k_win, v_win = rotary(kv[:, :, 2, 0], pos, ROPE_DIM, ROPE_THETA), kv[:, :, 2, 1]

    n_cmp = (S - CMP_BLOCK) // CMP_STRIDE + 1
    blk = np.arange(n_cmp)[:, None] * CMP_STRIDE + np.arange(CMP_BLOCK)[None, :]
    cmp_end = jnp.asarray(np.arange(n_cmp) * CMP_STRIDE + CMP_BLOCK - 1)

    def compress(t, w1, w2):
        tb = t[:, blk] + cmp_pe[None, None, :, None, :]
        hid = jax.nn.gelu(jnp.einsum('bnlgd,lde->bnge', tb, w1))
        return jnp.einsum('bnge,ef->bngf', hid, w2)

    k_c = rotary(compress(k_cmp, cmp_w1_k, cmp_w2_k), cmp_end, ROPE_DIM, ROPE_THETA)
    v_c = compress(v_cmp, cmp_w1_v, cmp_w2_v)
    mask_c = cmp_end[None, :] <= pos[:, None]
    s_c = jnp.einsum('bsgjd,bngd->bgjsn', q, k_c)
    p_c = masked_softmax(s_c, mask_c) * mask_c
    o_cmp = jnp.einsum('bgjsn,bngd->bsgjd', p_c.astype(v_c.dtype), v_c)

    n_sel = S // SEL_BLOCK
    cs0 = np.arange(n_cmp) * CMP_STRIDE
    ss0 = np.arange(n_sel) * SEL_BLOCK
    overlap = np.clip(np.minimum(cs0[:, None] + CMP_BLOCK, ss0[None, :] + SEL_BLOCK) - np.maximum(cs0[:, None], ss0[None, :]), 0, None)
    overlap = jnp.asarray((overlap / CMP_BLOCK).astype(np.float32))
    imp = jnp.einsum('bgjsn,nm->bgsm', p_c, overlap)
    cur = pos // SEL_BLOCK
    bid = jnp.arange(n_sel)
    forced = (bid[None, :] == 0) | (bid[None, :] == cur[:, None]) | (bid[None, :] == cur[:, None] - 1)
    future = bid[None, :] > cur[:, None]
    imp = jnp.where(forced, FORCE_SCORE, jnp.where(future, -FORCE_SCORE, imp))
    n_top = min(SEL_TOP_N, n_sel)
    _, sel_idx = lax.top_k(imp, n_top)

    k_blocks = k_slc.reshape(B, n_sel, SEL_BLOCK, G, Dh).transpose(0, 3, 1, 2, 4)
    v_blocks = v_slc.reshape(B, n_sel, SEL_BLOCK, G, Dh).transpose(0, 3, 1, 2, 4)
    n_qc = S // SEL_Q_CHUNK
    q_ch = q.reshape(B, n_qc, SEL_Q_CHUNK, G, J, Dh).swapaxes(0, 1)
    idx_ch = sel_idx.reshape(B, G, n_qc, SEL_Q_CHUNK, n_top).transpose(2, 0, 1, 3, 4)
    t_ch = pos.reshape(n_qc, SEL_Q_CHUNK)
    b_i = jnp.arange(B)[:, None, None, None]
    g_i = jnp.arange(G)[None, :, None, None]

    def sel_chunk(args):
        qc, ic, tc = args
        kg = k_blocks[b_i, g_i, ic]
        vg = v_blocks[b_i, g_i, ic]
        kpos = ic[..., None] * SEL_BLOCK + jnp.arange(SEL_BLOCK)
        m = (kpos <= tc[None, None, :, None, None]).reshape(B, G, 1, SEL_Q_CHUNK, n_top * SEL_BLOCK)
        s = jnp.einsum('bqgjd,bgqkld->bgjqkl', qc, kg).reshape(B, G, J, SEL_Q_CHUNK, n_top * SEL_BLOCK)
        p = masked_softmax(s, m).astype(vg.dtype).reshape(B, G, J, SEL_Q_CHUNK, n_top, SEL_BLOCK)
        return jnp.einsum('bgjqkl,bgqkld->bqgjd', p, vg)

    o_slc = lax.map(sel_chunk, (q_ch, idx_ch, t_ch)).swapaxes(0, 1).reshape(B, S, G, J, Dh)

    n_wb = S // WIN_Q_BLOCK
    span = WINDOW + WIN_Q_BLOCK
    k_pad = jnp.pad(k_win, ((0, 0), (WINDOW, 0), (0, 0), (0, 0)))
    v_pad = jnp.pad(v_win, ((0, 0), (WINDOW, 0), (0, 0), (0, 0)))
    q_wb = q.reshape(B, n_wb, WIN_Q_BLOCK, G, J, Dh).swapaxes(0, 1)

    def win_block(args):
        qb, c = args
        start = c * WIN_Q_BLOCK
        kb = lax.dynamic_slice_in_dim(k_pad, start, span, axis=1)
        vb = lax.dynamic_slice_in_dim(v_pad, start, span, axis=1)
        tq = start + jnp.arange(WIN_Q_BLOCK)
        tk = start - WINDOW + jnp.arange(span)
        m = (tk[None, :] <= tq[:, None]) & (tk[None, :] > tq[:, None] - WINDOW) & (tk[None, :] >= 0)
        s = jnp.einsum('bqgjd,bkgd->bgjqk', qb, kb)
        p = masked_softmax(s, m).astype(vb.dtype)
        return jnp.einsum('bgjqk,bkgd->bqgjd', p, vb)

    o_win = lax.map(win_block, (q_wb, jnp.arange(n_wb))).swapaxes(0, 1).reshape(B, S, G, J, Dh)

    g = jax.nn.sigmoid(zg.reshape(B, S, G, J, 3))
    o = g[..., 0:1] * o_cmp + g[..., 1:2] * o_slc + g[..., 2:3] * o_win
    return o.reshape(B, S, NSA_HEADS * Dh)


def retention_mixer(z, gn_g):
    B, S, _ = z.shape
    H, dk, C = RET_HEADS, RET_HEAD_DIM, RET_CHUNK
    zq, zk, zv, zg = jnp.split(z, 4, axis=-1)
    pos = jnp.arange(S)
    q = rotary(zq.reshape(B, S, H, dk), pos, dk, RET_THETA).astype(jnp.float32)
    k = rotary(zk.reshape(B, S, H, dk), pos, dk, RET_THETA).astype(jnp.float32) * (dk ** -0.5)
    v = zv.reshape(B, S, H, dk).astype(jnp.float32)
    n_c = S // C

    def to_chunks(t):
        return t.reshape(B, n_c, C, H, dk).transpose(1, 0, 3, 2, 4)

    log_g = np.log1p(-np.exp2(-5.0 - np.arange(H))).astype(np.float32)
    n = np.arange(C, dtype=np.float32)
    diff = n[:, None] - n[None, :]
    intra = jnp.asarray(np.where(diff >= 0, np.exp(np.maximum(diff, 0.0)[None] * log_g[:, None, None]), 0.0).astype(np.float32))
    q_dec = jnp.asarray(np.exp((n[None, :] + 1.0) * log_g[:, None]).astype(np.float32))
    k_dec = jnp.asarray(np.exp((C - 1.0 - n[None, :]) * log_g[:, None]).astype(np.float32))
    c_dec = jnp.asarray(np.exp(C * log_g).astype(np.float32))

    def step(R, inp):
        qc, kc, vc = inp
        scores = jnp.einsum('bhnd,bhmd->bhnm', qc, kc) * intra
        inner = jnp.einsum('bhnm,bhme->bhne', scores, vc)
        cross = jnp.einsum('bhnd,bhde->bhne', qc * q_dec[:, :, None], R)
        R = R * c_dec[:, None, None] + jnp.einsum('bhmd,bhme->bhde', kc * k_dec[:, :, None], vc)
        return R, inner + cross

    R0 = jnp.zeros((B, H, dk, dk), jnp.float32)
    _, o = lax.scan(step, R0, (to_chunks(q), to_chunks(k), to_chunks(v)))
    o = o.transpose(1, 0, 3, 2, 4).reshape(B, S, H, dk)
    mu = jnp.mean(o, axis=-1, keepdims=True)
    var = jnp.mean(jnp.square(o - mu), axis=-1, keepdims=True)
    o = ((o - mu) * lax.rsqrt(var + EPS)).reshape(B, S, H * dk) * gn_g.astype(jnp.float32)
    return (o * jax.nn.silu(zg.astype(jnp.float32))).astype(z.dtype)


def pool_mixer(u, pool_w, pool_scale):
    B, S, _ = u.shape
    ug = u.reshape(B, S, POOL_GROUPS, POOL_GROUP_DIM).astype(jnp.float32)
    cs = jnp.pad(jnp.cumsum(ug, axis=1), ((0, 0), (1, 0), (0, 0), (0, 0)))
    t = np.arange(S)
    win = np.array(POOL_WINDOWS)
    lo = np.maximum(t[:, None] + 1 - win[None, :], 0)
    cnt = jnp.asarray((t[:, None] + 1 - lo).astype(np.float32))
    win_sum = cs[:, 1:] - cs[:, lo, np.arange(POOL_GROUPS)[None, :]]
    pooled = (win_sum / cnt[None, :, :, None] - ug).astype(u.dtype)
    y = jnp.einsum('bsgc,gcd->bsgd', pooled, pool_w).reshape(B, S, BRANCH_WIDTH)
    return y * pool_scale


def hybrid_layer(x, norm1_g, w_in, b_gate, sc_conv, cmp_pe, cmp_w1_k, cmp_w2_k, cmp_w1_v, cmp_w2_v, ret_gn_g, pool_w, pool_scale, w_branch, w_o, norm2_g, w_up, ffn_conv, w_down):
    B, S, D = x.shape
    h = rms_norm(x, norm1_g)
    z = h @ w_in
    z_sc, z_q, z_kv, z_ng, z_ret, z_pool, z_gate = jnp.split(z, np.cumsum(IN_SPLITS)[:-1].tolist(), axis=-1)
    y_a = short_conv_mixer(z_sc, sc_conv)
    y_b = nsa_mixer(z_q, z_kv, z_ng, cmp_pe, cmp_w1_k, cmp_w2_k, cmp_w1_v, cmp_w2_v)
    y_c = retention_mixer(z_ret, ret_gn_g)
    y_d = pool_mixer(z_pool, pool_w, pool_scale)
    branches = jnp.stack([y_a, y_b, y_c, y_d], axis=2)
    gates = jax.nn.sigmoid((z_gate + b_gate).reshape(B, S, N_BRANCH, D))
    merged = jnp.sum(jnp.einsum('bsnc,ncd->bsnd', branches, w_branch) * gates, axis=2)
    x = x + merged @ w_o
    h = rms_norm(x, norm2_g)
    u_gate, u_val = jnp.split(h @ w_up, 2, axis=-1)
    return x + (jax.nn.silu(causal_dwconv(u_gate, ffn_conv)) * u_val) @ w_down


def setup_inputs(seed: int = 0) -> dict:
    key = jax.random.key(seed)
    ks = jax.random.split(key, 21)
    f32 = jnp.float32
    Dh = NSA_HEAD_DIM

    def nrm(k, shape, scale):
        return jax.random.normal(k, shape, f32) * scale

    def gain(k, shape):
        return 1.0 + 0.02 * jax.random.normal(k, shape, f32)

    return {
        'x': nrm(ks[0], (BATCH, SEQ, D_MODEL), 1.0),
        'norm1_g': gain(ks[1], (DEPTH, D_MODEL)),
        'w_in': nrm(ks[2], (DEPTH, D_MODEL, D_IN), D_MODEL ** -0.5),
        'b_gate': nrm(ks[3], (DEPTH, N_BRANCH * D_MODEL), 0.02),
        'sc_conv': nrm(ks[4], (DEPTH, CONV_TAPS, SC_WIDTH), CONV_TAPS ** -0.5),
        'cmp_pe': nrm(ks[5], (DEPTH, CMP_BLOCK, Dh), 0.1),
        'cmp_w1_k': nrm(ks[6], (DEPTH, CMP_BLOCK, Dh, Dh), (CMP_BLOCK * Dh) ** -0.5),
        'cmp_w2_k': nrm(ks[7], (DEPTH, Dh, Dh), Dh ** -0.5),
        'cmp_w1_v': nrm(ks[8], (DEPTH, CMP_BLOCK, Dh, Dh), (CMP_BLOCK * Dh) ** -0.5),
        'cmp_w2_v': nrm(ks[9], (DEPTH, Dh, Dh), Dh ** -0.5),
        'ret_gn_g': gain(ks[10], (DEPTH, BRANCH_WIDTH)),
        'pool_w': nrm(ks[11], (DEPTH, POOL_GROUPS, POOL_GROUP_DIM, POOL_GROUP_DIM), POOL_GROUP_DIM ** -0.5),
        'pool_scale': gain(ks[12], (DEPTH, BRANCH_WIDTH)),
        'w_branch': nrm(ks[13], (DEPTH, N_BRANCH, BRANCH_WIDTH, D_MODEL), BRANCH_WIDTH ** -0.5),
        'w_o': nrm(ks[14], (DEPTH, D_MODEL, D_MODEL), D_MODEL ** -0.5),
        'norm2_g': gain(ks[15], (DEPTH, D_MODEL)),
        'w_up': nrm(ks[16], (DEPTH, D_MODEL, 2 * D_FF), D_MODEL ** -0.5),
        'ffn_conv': nrm(ks[17], (DEPTH, CONV_TAPS, D_FF), CONV_TAPS ** -0.5),
        'w_down': nrm(ks[18], (DEPTH, D_FF, D_MODEL), D_FF ** -0.5),
        'final_norm_g': gain(ks[19], (D_MODEL,)),
    }


def reference(x, norm1_g, w_in, b_gate, sc_conv, cmp_pe, cmp_w1_k, cmp_w2_k, cmp_w1_v, cmp_w2_v, ret_gn_g, pool_w, pool_scale, w_branch, w_o, norm2_g, w_up, ffn_conv, w_down, final_norm_g):
    for l in range(DEPTH):
        x = hybrid_layer(x, norm1_g[l], w_in[l], b_gate[l], sc_conv[l], cmp_pe[l], cmp_w1_k[l], cmp_w2_k[l], cmp_w1_v[l], cmp_w2_v[l], ret_gn_g[l], pool_w[l], pool_scale[l], w_branch[l], w_o[l], norm2_g[l], w_up[l], ffn_conv[l], w_down[l])
    return rms_norm(x, final_norm_g)
```

```python
import functools

import numpy as np
import jax
import jax.numpy as jnp
from jax import lax
from jax.experimental import pallas as pl
from jax.experimental.pallas import tpu as pltpu

F32 = jnp.float32
BF16 = jnp.bfloat16

D_MODEL = 1024
SEQ = 2048
BRANCH = 512
N_BRANCH = 4
CONV_TAPS = 3
NSA_HEADS = 8
NSA_GROUPS = 2
NSA_HPG = NSA_HEADS // NSA_GROUPS
NSA_DH = BRANCH // NSA_HEADS
CMP_BLOCK = 32
CMP_STRIDE = 16
N_CMP = (SEQ - CMP_BLOCK) // CMP_STRIDE + 1
N_CMP_PAD = SEQ // CMP_STRIDE
SEL_BLOCK = 64
N_SEL = SEQ // SEL_BLOCK
SEL_TOP_N = 16
WINDOW = 512
ROPE_THETA = 500000.0
ROPE_DIM = NSA_DH // 4
FORCE_SCORE = 1.0e4
RET_HEADS = 4
RET_DK = BRANCH // RET_HEADS
RET_CHUNK = 128
RET_THETA = 10000.0
POOL_WINDOWS = (2, 4, 8, 16)
POOL_GROUP_DIM = BRANCH // len(POOL_WINDOWS)
D_FF = 2816
EPS = 1e-6
NEG_INF = -1e30

SC_OFF = 0
Q_OFF = 3 * BRANCH
KV_OFF = Q_OFF + BRANCH
NG_OFF = KV_OFF + 3 * 2 * NSA_GROUPS * NSA_DH
NG_COLS = 3 * NSA_HEADS
NG_PAD = 128
RET_OFF = NG_OFF + NG_PAD
POOL_OFF = RET_OFF + 4 * BRANCH
MAIN_COLS = POOL_OFF + BRANCH
GATE_OFF_ORIG = NG_OFF + NG_COLS + 4 * BRANCH + BRANCH

TOK_TILE = 512
TILES_PER_SEQ = SEQ // TOK_TILE
CONV_HALO = 8
POOL_HALO = 16
ATT_TQ = 256
ATT_TK = 256
WIN_SPAN = WINDOW + ATT_TQ
CMP_TQ = 512
VMEM_LIMIT = 56 * 1024 * 1024


def _const_spec(shape):
    n = len(shape)
    return pl.BlockSpec(shape, lambda *_: (0,) * n, pipeline_mode=pl.Buffered(1))


def _rms(x, g):
    ms = jnp.mean(x * x, axis=-1, keepdims=True)
    return x * lax.rsqrt(ms + EPS) * g


def _dot(a, b):
    return jnp.dot(a, b, preferred_element_type=F32)


def _dot_nt(a, b):
    return lax.dot_general(a, b, (((1,), (1,)), ((), ())), preferred_element_type=F32)


def _inproj_kernel(x_ref, g_ref, w_ref, scw_ref, nc_ref, nsa_ref, nsb_ref, rc_ref, rs_ref, pw_ref, ps_ref,
                   ya_ref, q_ref, kvc_ref, kv2_ref, ng_ref, rq_ref, rg_ref, yd_ref, ch_buf, u_buf):
    T = TOK_TILE
    i = pl.program_id(0)

    @pl.when(i % TILES_PER_SEQ == 0)
    def _():
        ch_buf[0:CONV_HALO, :] = jnp.zeros((CONV_HALO, BRANCH), F32)
        u_buf[0:POOL_HALO, :] = jnp.zeros((POOL_HALO, BRANCH), F32)

    h = _rms(x_ref[...], g_ref[...]).astype(BF16)

    def proj(off, width):
        return _dot(h, w_ref[:, off:off + width])

    zb = proj(SC_OFF, BRANCH)
    ch_buf[CONV_HALO:CONV_HALO + T, :] = proj(SC_OFF + BRANCH, BRANCH) * proj(SC_OFF + 2 * BRANCH, BRANCH)
    conv = scw_ref[CONV_TAPS - 1:CONV_TAPS, :] * ch_buf[CONV_HALO:CONV_HALO + T, :]
    for k in range(1, CONV_TAPS):
        conv = conv + scw_ref[CONV_TAPS - 1 - k:CONV_TAPS - k, :] * ch_buf[CONV_HALO - k:CONV_HALO - k + T, :]
    ya_ref[...] = (zb * conv).astype(BF16)
    ch_buf[0:CONV_HALO, :] = ch_buf[T:T + CONV_HALO, :]

    nc, nsa, nsb = nc_ref[...], nsa_ref[...], nsb_ref[...]

    def rope_nsa(x):
        return x * nc + pltpu.roll(x, 128 - ROPE_DIM // 2, 1) * nsa + pltpu.roll(x, ROPE_DIM // 2, 1) * nsb

    zq = proj(Q_OFF, BRANCH)
    for c in range(BRANCH // 128):
        q_ref[:, c * 128:(c + 1) * 128] = (rope_nsa(zq[:, c * 128:(c + 1) * 128]) * (NSA_DH ** -0.5)).astype(BF16)
    zkv = proj(KV_OFF, 768)
    kvc_ref[...] = zkv[:, 0:256]
    kv2_ref[:, 0:128] = rope_nsa(zkv[:, 256:384]).astype(BF16)
    kv2_ref[:, 128:256] = zkv[:, 384:512].astype(BF16)
    kv2_ref[:, 256:384] = rope_nsa(zkv[:, 512:640]).astype(BF16)
    kv2_ref[:, 384:512] = zkv[:, 640:768].astype(BF16)
    ng_ref[...] = jax.nn.sigmoid(proj(NG_OFF, NG_PAD))

    rc, rs = rc_ref[...], rs_ref[...]

    def rope_ret(x):
        return x * rc + pltpu.roll(x, RET_DK // 2, 1) * rs

    zr = proj(RET_OFF, BRANCH)
    for c in range(RET_HEADS):
        rq_ref[:, c * 128:(c + 1) * 128] = rope_ret(zr[:, c * 128:(c + 1) * 128]).astype(BF16)
    zr = proj(RET_OFF + BRANCH, BRANCH)
    for c in range(RET_HEADS):
        rq_ref[:, BRANCH + c * 128:BRANCH + (c + 1) * 128] = (
            rope_ret(zr[:, c * 128:(c + 1) * 128]) * (RET_DK ** -0.5)).astype(BF16)
    rq_ref[:, 2 * BRANCH:3 * BRANCH] = proj(RET_OFF + 2 * BRANCH, BRANCH).astype(BF16)
    rg_ref[...] = proj(RET_OFF + 3 * BRANCH, BRANCH)

    zu = proj(POOL_OFF, BRANCH)
    u_buf[POOL_HALO:POOL_HALO + T, :] = zu
    pos = (i % TILES_PER_SEQ) * T + lax.broadcasted_iota(jnp.int32, (T, 1), 0)
    for gi, win in enumerate(POOL_WINDOWS):
        c0, c1 = gi * POOL_GROUP_DIM, (gi + 1) * POOL_GROUP_DIM
        s = zu[:, c0:c1]
        for k in range(1, win):
            s = s + u_buf[POOL_HALO - k:POOL_HALO - k + T, c0:c1]
        cnt = jnp.minimum(pos + 1, win).astype(F32)
        pooled = s / cnt - zu[:, c0:c1]
        y = _dot(pooled.astype(BF16), pw_ref[gi]) * ps_ref[:, c0:c1]
        yd_ref[:, c0:c1] = y.astype(BF16)
    u_buf[0:POOL_HALO, :] = u_buf[T:T + POOL_HALO, :]


def _inproj(x2, g, w_main, scw, tabs, pool_w, pool_scale):
    n = x2.shape[0]
    T = TOK_TILE
    tok = lambda width: pl.BlockSpec((T, width), lambda i: (i, 0))
    tab = pl.BlockSpec((T, 128), lambda i: (i % TILES_PER_SEQ, 0))
    out_shapes = (
        jax.ShapeDtypeStruct((n, BRANCH), BF16),
        jax.ShapeDtypeStruct((n, BRANCH), BF16),
        jax.ShapeDtypeStruct((n, 256), F32),
        jax.ShapeDtypeStruct((n, 512), BF16),
        jax.ShapeDtypeStruct((n, NG_PAD), F32),
        jax.ShapeDtypeStruct((n, 3 * BRANCH), BF16),
        jax.ShapeDtypeStruct((n, BRANCH), F32),
        jax.ShapeDtypeStruct((n, BRANCH), BF16),
    )
    return pl.pallas_call(
        _inproj_kernel,
        name="inproj",
        grid=(n // T,),
        in_specs=[tok(D_MODEL), _const_spec((1, D_MODEL)), _const_spec((D_MODEL, MAIN_COLS)),
                  _const_spec((CONV_TAPS, BRANCH)), tab, tab, tab, tab, tab,
                  _const_spec((len(POOL_WINDOWS), POOL_GROUP_DIM, POOL_GROUP_DIM)), _const_spec((1, BRANCH))],
        out_specs=[tok(BRANCH), tok(BRANCH), tok(256), tok(512), tok(NG_PAD), tok(3 * BRANCH), tok(BRANCH),
                   tok(BRANCH)],
        out_shape=out_shapes,
        scratch_shapes=[pltpu.VMEM((CONV_HALO + T, BRANCH), F32), pltpu.VMEM((POOL_HALO + T, BRANCH), F32)],
        compiler_params=pltpu.CompilerParams(dimension_semantics=("arbitrary",), vmem_limit_bytes=VMEM_LIMIT),
    )(x2, g, w_main, scw, *tabs, pool_w, pool_scale)


def _compress_kernel(x_ref, pea_ref, peb_ref, w1a_ref, w1b_ref, w2_ref, cc_ref, csa_ref, csb_ref, out_ref):
    pea, peb = pea_ref[...], peb_ref[...]
    for idx in range(2 * NSA_GROUPS):
        kv = idx // NSA_GROUPS
        x = x_ref[0, idx]
        first = _dot((x + pea).astype(BF16), w1a_ref[kv])
        second = _dot((x + peb).astype(BF16), w1b_ref[kv])
        hid = jax.nn.gelu(first + pltpu.roll(second, N_CMP_PAD - 1, 0))
        y = _dot(hid.astype(BF16), w2_ref[kv])
        if kv == 0:
            y = (y * cc_ref[...] + pltpu.roll(y, 128 - ROPE_DIM // 2, 1) * csa_ref[...]
                 + pltpu.roll(y, ROPE_DIM // 2, 1) * csb_ref[...])
        out_ref[0, idx] = y[:, 0:NSA_DH].astype(BF16)


def _compress(xc, pea, peb, w1a, w1b, w2p, ctabs):
    b = xc.shape[0]
    return pl.pallas_call(
        _compress_kernel,
        name="compress",
        grid=(b,),
        in_specs=[pl.BlockSpec((1, 4, N_CMP_PAD, 1024), lambda i: (i, 0, 0, 0)),
                  _const_spec((1, 1024)), _const_spec((1, 1024)),
                  _const_spec((2, 1024, NSA_DH)), _const_spec((2, 1024, NSA_DH)), _const_spec((2, NSA_DH, 128)),
                  _const_spec((N_CMP_PAD, 128)), _const_spec((N_CMP_PAD, 128)), _const_spec((N_CMP_PAD, 128))],
        out_specs=pl.BlockSpec((1, 4, N_CMP_PAD, NSA_DH), lambda i: (i, 0, 0, 0)),
        out_shape=jax.ShapeDtypeStruct((b, 4, N_CMP_PAD, NSA_DH), BF16),
        compiler_params=pltpu.CompilerParams(dimension_semantics=("arbitrary",), vmem_limit_bytes=VMEM_LIMIT),
    )(xc, pea, peb, w1a, w1b, w2p, *ctabs)


def _cmp_attn_kernel(q_ref, kc_ref, vc_ref, ovt_ref, o_ref, sb_ref):
    R = CMP_TQ
    r = pl.program_id(2)
    kc, vc = kc_ref[0, 0], vc_ref[0, 0]
    t_rows = r * R + lax.broadcasted_iota(jnp.int32, (R, N_CMP_PAD), 0)
    blk_end = lax.broadcasted_iota(jnp.int32, (R, N_CMP_PAD), 1) * CMP_STRIDE + (CMP_BLOCK - 1)
    valid = blk_end <= t_rows
    psum = jnp.zeros((R, N_CMP_PAD), F32)
    for j in range(NSA_HPG):
        s = _dot_nt(q_ref[0, :, j * NSA_DH:(j + 1) * NSA_DH], kc)
        sm = jnp.where(valid, s, NEG_INF)
        e = jnp.exp(sm - jnp.max(sm, axis=-1, keepdims=True))
        p = jnp.where(valid, e, 0.0) / jnp.sum(e, axis=-1, keepdims=True)
        o_ref[0, :, j * NSA_DH:(j + 1) * NSA_DH] = _dot(p.astype(BF16), vc)
        psum = psum + p
    imp = lax.dot_general(ovt_ref[...], psum, (((1,), (1,)), ((), ())), preferred_element_type=F32,
                          precision=lax.Precision.HIGHEST)
    cur = (r * R + lax.broadcasted_iota(jnp.int32, (N_SEL, R), 1)) // SEL_BLOCK
    bid = lax.broadcasted_iota(jnp.int32, (N_SEL, R), 0)
    forced = (bid == 0) | (bid == cur) | (bid == cur - 1)
    imp = jnp.where(forced, FORCE_SCORE, jnp.where(bid > cur, -FORCE_SCORE, imp))
    rank = jnp.zeros((N_SEL, R), F32)
    for m in range(N_SEL):
        a = imp[m:m + 1, :]
        before = jnp.where(bid > m, jnp.where(a >= imp, 1.0, 0.0), jnp.where(a > imp, 1.0, 0.0))
        rank = rank + before
    bias_t = jnp.where(rank < SEL_TOP_N, 0.0, NEG_INF)
    bias = bias_t.T
    sb_ref[0, 0] = jnp.concatenate([bias, jnp.zeros((R, NSA_DH - N_SEL), F32)], axis=1).astype(BF16)


def _cmp_attn(q3, kcvc, ovt):
    b = q3.shape[0]
    R = CMP_TQ
    gw = NSA_HPG * NSA_DH
    return pl.pallas_call(
        _cmp_attn_kernel,
        name="cmp_attn",
        grid=(b, NSA_GROUPS, SEQ // R),
        in_specs=[pl.BlockSpec((1, R, gw), lambda i, g, r: (i, r, g)),
                  pl.BlockSpec((1, 1, N_CMP_PAD, NSA_DH), lambda i, g, r: (i, g, 0, 0)),
                  pl.BlockSpec((1, 1, N_CMP_PAD, NSA_DH), lambda i, g, r: (i, NSA_GROUPS + g, 0, 0)),
                  _const_spec((N_SEL, N_CMP_PAD))],
        out_specs=[pl.BlockSpec((1, R, gw), lambda i, g, r: (i, r, g)),
                   pl.BlockSpec((1, 1, R, NSA_DH), lambda i, g, r: (i, g, r, 0))],
        out_shape=(jax.ShapeDtypeStruct((b, SEQ, BRANCH), F32),
                   jax.ShapeDtypeStruct((b, NSA_GROUPS, SEQ, NSA_DH), BF16)),
        compiler_params=pltpu.CompilerParams(dimension_semantics=("arbitrary",) * 3, vmem_limit_bytes=VMEM_LIMIT),
    )(q3, kcvc, kcvc, ovt)


def _sel_win_kernel(q_ref, kv_ref, sb_ref, e_ref, oc_ref, ng_ref, y_ref, ka_ref):
    TQ, TK = ATT_TQ, ATT_TK
    qi = pl.program_id(1)
    q0 = qi * TQ

    @pl.when(qi == 0)
    def _():
        for g in range(NSA_GROUPS):
            ka_ref[g, :, 0:NSA_DH] = kv_ref[0, :, g * NSA_DH:(g + 1) * NSA_DH]
            ka_ref[g, :, NSA_DH:2 * NSA_DH] = e_ref[...]

    row = lax.broadcasted_iota(jnp.int32, (TQ, TK), 0)
    col = lax.broadcasted_iota(jnp.int32, (TQ, TK), 1)
    causal = col <= row
    start = pl.multiple_of(jnp.maximum(q0 - WINDOW, 0), TQ)
    qpos = q0 + lax.broadcasted_iota(jnp.int32, (TQ, WIN_SPAN), 0)
    kpos = start + lax.broadcasted_iota(jnp.int32, (TQ, WIN_SPAN), 1)
    wbias = jnp.where(kpos <= qpos, jnp.where(kpos > qpos - WINDOW, 0.0, NEG_INF), NEG_INF)
    gates = ng_ref[0]

    outs = []
    for h in range(NSA_HEADS):
        g = h // NSA_HPG
        lanes = slice(g * NSA_DH, (g + 1) * NSA_DH)
        qh = q_ref[0, :, h * NSA_DH:(h + 1) * NSA_DH]
        qa = jnp.concatenate([qh, sb_ref[0, g]], axis=1)

        def tile(t, carry, diagonal, g=g, qa=qa, lanes=lanes):
            m, l, acc = carry
            k0 = pl.multiple_of(t * TK, TK)
            s = _dot_nt(qa, ka_ref[g, pl.ds(k0, TK), :])
            if diagonal:
                s = jnp.where(causal, s, NEG_INF)
            m_new = jnp.maximum(m, jnp.max(s, axis=-1, keepdims=True))
            alpha = jnp.exp(m - m_new)
            p = jnp.exp(s - m_new)
            l = alpha * l + jnp.sum(p, axis=-1, keepdims=True)
            v = kv_ref[0, pl.ds(k0, TK), 128 + g * NSA_DH:128 + (g + 1) * NSA_DH]
            acc = alpha * acc + _dot(p.astype(BF16), v)
            return m_new, l, acc

        init = (jnp.full((TQ, 1), NEG_INF, F32), jnp.zeros((TQ, 1), F32), jnp.zeros((TQ, NSA_DH), F32))
        carry = lax.fori_loop(0, qi, functools.partial(tile, diagonal=False), init)
        _, l, acc = tile(qi, carry, True)
        o_sel = acc / l

        kw = kv_ref[0, pl.ds(start, WIN_SPAN), 256 + g * NSA_DH:256 + (g + 1) * NSA_DH]
        vw = kv_ref[0, pl.ds(start, WIN_SPAN), 384 + g * NSA_DH:384 + (g + 1) * NSA_DH]
        s = _dot_nt(qh, kw) + wbias
        p = jnp.exp(s - jnp.max(s, axis=-1, keepdims=True))
        o_win = _dot(p.astype(BF16), vw) / jnp.sum(p, axis=-1, keepdims=True)

        o_cmp = oc_ref[0, :, h * NSA_DH:(h + 1) * NSA_DH]
        outs.append(gates[:, 3 * h:3 * h + 1] * o_cmp + gates[:, 3 * h + 1:3 * h + 2] * o_sel
                    + gates[:, 3 * h + 2:3 * h + 3] * o_win)
    y_ref[0] = jnp.concatenate(outs, axis=1).astype(BF16)


def _sel_win(q3, kv3, selbias, e_tab, ocmp, ng3):
    b = q3.shape[0]
    TQ = ATT_TQ
    return pl.pallas_call(
        _sel_win_kernel,
        name="sel_win_attn",
        grid=(b, SEQ // TQ),
        in_specs=[pl.BlockSpec((1, TQ, BRANCH), lambda i, t: (i, t, 0)),
                  pl.BlockSpec((1, SEQ, 512), lambda i, t: (i, 0, 0)),
                  pl.BlockSpec((1, NSA_GROUPS, TQ, NSA_DH), lambda i, t: (i, 0, t, 0)),
                  _const_spec((SEQ, NSA_DH)),
                  pl.BlockSpec((1, TQ, BRANCH), lambda i, t: (i, t, 0)),
                  pl.BlockSpec((1, TQ, NG_PAD), lambda i, t: (i, t, 0))],
        out_specs=pl.BlockSpec((1, TQ, BRANCH), lambda i, t: (i, t, 0)),
        out_shape=jax.ShapeDtypeStruct((b, SEQ, BRANCH), BF16),
        scratch_shapes=[pltpu.VMEM((NSA_GROUPS, SEQ, 2 * NSA_DH), BF16)],
        compiler_params=pltpu.CompilerParams(dimension_semantics=("arbitrary", "arbitrary"),
                                             vmem_limit_bytes=VMEM_LIMIT),
    )(q3, kv3, selbias, e_tab, ocmp, ng3)


def _retention_kernel(q_ref, k_ref, v_ref, zg_ref, intra_ref, qd_ref, kd_ref, cd_ref, gn_ref, y_ref):
    C = RET_CHUNK
    intra, qd, kd = intra_ref[0], qd_ref[0], kd_ref[0]
    cd = cd_ref[0, 0:1, :]
    gn = gn_ref[...]
    state = jnp.zeros((RET_DK, RET_DK), F32)
    for c in range(SEQ // C):
        rows = slice(c * C, (c + 1) * C)
        qc, kc, vc = q_ref[0, rows, :], k_ref[0, rows, :], v_ref[0, rows, :]
        scores = _dot_nt(qc, kc) * intra
        inner = _dot(scores.astype(BF16), vc)
        cross = _dot((qc.astype(F32) * qd).astype(BF16), state.astype(BF16))
        kdec = (kc.astype(F32) * kd).astype(BF16)
        state = state * cd + lax.dot_general(kdec, vc, (((0,), (0,)), ((), ())), preferred_element_type=F32)
        o = inner + cross
        mu = jnp.mean(o, axis=-1, keepdims=True)
        d = o - mu
        var = jnp.mean(d * d, axis=-1, keepdims=True)
        o = d * lax.rsqrt(var + EPS) * gn
        y_ref[0, rows, :] = (o * jax.nn.silu(zg_ref[0, rows, :])).astype(BF16)


def _retention(rq3, rg3, intra, qd, kd, cd, gn):
    b = rq3.shape[0]
    hspec = lambda off: pl.BlockSpec((1, SEQ, RET_DK), lambda i, h: (i, 0, off + h))
    tspec = pl.BlockSpec((1, RET_CHUNK, RET_DK), lambda i, h: (h, 0, 0))
    return pl.pallas_call(
        _retention_kernel,
        name="retention",
        grid=(b, RET_HEADS),
        in_specs=[hspec(0), hspec(RET_HEADS), hspec(2 * RET_HEADS), hspec(0), tspec, tspec, tspec,
                  pl.BlockSpec((1, 8, RET_DK), lambda i, h: (h, 0, 0)),
                  pl.BlockSpec((1, RET_DK), lambda i, h: (0, h))],
        out_specs=hspec(0),
        out_shape=jax.ShapeDtypeStruct((b, SEQ, BRANCH), BF16),
        compiler_params=pltpu.CompilerParams(dimension_semantics=("arbitrary", "arbitrary"),
                                             vmem_limit_bytes=VMEM_LIMIT),
    )(rq3, rq3, rq3, rg3, intra, qd, kd, cd, gn)


def _merge_kernel(x_ref, g_ref, wg_ref, bg_ref, ya_ref, yb_ref, yc_ref, yd_ref, wb_ref, wo_ref, o_ref):
    x = x_ref[...]
    h = _rms(x, g_ref[...]).astype(BF16)
    merged = None
    for n, y_ref in enumerate((ya_ref, yb_ref, yc_ref, yd_ref)):
        gate = jax.nn.sigmoid(_dot(h, wg_ref[:, n * D_MODEL:(n + 1) * D_MODEL])
                              + bg_ref[:, n * D_MODEL:(n + 1) * D_MODEL])
        term = _dot(y_ref[...], wb_ref[n]) * gate
        merged = term if merged is None else merged + term
    o_ref[...] = x + _dot(merged.astype(BF16), wo_ref[...])


def _merge(x2, g, wg, bg, ya, yb, yc, yd, wb, wo):
    n = x2.shape[0]
    T = TOK_TILE
    tok = lambda width: pl.BlockSpec((T, width), lambda i: (i, 0))
    return pl.pallas_call(
        _merge_kernel,
        name="merge",
        grid=(n // T,),
        in_specs=[tok(D_MODEL), _const_spec((1, D_MODEL)), _const_spec((D_MODEL, N_BRANCH * D_MODEL)),
                  _const_spec((1, N_BRANCH * D_MODEL)), tok(BRANCH), tok(BRANCH), tok(BRANCH), tok(BRANCH),
                  _const_spec((N_BRANCH, BRANCH, D_MODEL)), _const_spec((D_MODEL, D_MODEL))],
        out_specs=tok(D_MODEL),
        out_shape=jax.ShapeDtypeStruct((n, D_MODEL), F32),
        compiler_params=pltpu.CompilerParams(dimension_semantics=("arbitrary",), vmem_limit_bytes=VMEM_LIMIT),
    )(x2, g, wg, bg, ya, yb, yc, yd, wb, wo)


FF_CHUNK = 256


def _mlp_kernel(x_ref, g_ref, wup_ref, cw_ref, wdn_ref, fg_ref, o_ref, ug_buf, act_buf, *, final_norm):
    T = TOK_TILE
    i = pl.program_id(0)

    @pl.when(i % TILES_PER_SEQ == 0)
    def _():
        ug_buf[0:CONV_HALO, :] = jnp.zeros((CONV_HALO, D_FF), F32)

    x = x_ref[...]
    h = _rms(x, g_ref[...]).astype(BF16)
    for c in range(D_FF // FF_CHUNK):
        cols = slice(c * FF_CHUNK, (c + 1) * FF_CHUNK)
        ug_buf[CONV_HALO:CONV_HALO + T, cols] = _dot(h, wup_ref[:, cols])
        conv = cw_ref[CONV_TAPS - 1:CONV_TAPS, cols] * ug_buf[CONV_HALO:CONV_HALO + T, cols]
        for k in range(1, CONV_TAPS):
            conv = conv + cw_ref[CONV_TAPS - 1 - k:CONV_TAPS - k, cols] * ug_buf[CONV_HALO - k:CONV_HALO - k + T, cols]
        val = _dot(h, wup_ref[:, D_FF + c * FF_CHUNK:D_FF + (c + 1) * FF_CHUNK])
        act_buf[:, cols] = (jax.nn.silu(conv) * val).astype(BF16)
    ug_buf[0:CONV_HALO, :] = ug_buf[T:T + CONV_HALO, :]
    y = x + _dot(act_buf[...], wdn_ref[...])
    if final_norm:
        y = _rms(y, fg_ref[...])
    o_ref[...] = y


def _mlp(x2, g, wup, cw, wdn, fg, final_norm):
    n = x2.shape[0]
    T = TOK_TILE
    tok = pl.BlockSpec((T, D_MODEL), lambda i: (i, 0))
    return pl.pallas_call(
        functools.partial(_mlp_kernel, final_norm=final_norm),
        name="mlp",
        grid=(n // T,),
        in_specs=[tok, _const_spec((1, D_MODEL)), _const_spec((D_MODEL, 2 * D_FF)), _const_spec((CONV_TAPS, D_FF)),
                  _const_spec((D_FF, D_MODEL)), _const_spec((1, D_MODEL))],
        out_specs=tok,
        out_shape=jax.ShapeDtypeStruct((n, D_MODEL), F32),
        scratch_shapes=[pltpu.VMEM((CONV_HALO + T, D_FF), F32), pltpu.VMEM((T, D_FF), BF16)],
        compiler_params=pltpu.CompilerParams(dimension_semantics=("arbitrary",), vmem_limit_bytes=VMEM_LIMIT),
    )(x2, g, wup, cw, wdn, fg)


def _rope_angles(pos, half, theta):
    inv_freq = np.power(np.float32(theta), -np.arange(half, dtype=np.float32) / np.float32(half))
    ang = pos.astype(F32)[:, None] * jnp.asarray(inv_freq)[None, :]
    return jnp.cos(ang), jnp.sin(ang)


def _nsa_rope_tables(pos):
    half = ROPE_DIM // 2
    cos, sin = _rope_angles(pos, half, ROPE_THETA)
    n = pos.shape[0]
    ones = jnp.ones((n, NSA_DH - ROPE_DIM), F32)
    zeros = jnp.zeros((n, NSA_DH - ROPE_DIM), F32)
    zh = jnp.zeros((n, half), F32)
    c = jnp.concatenate([cos, cos, ones], axis=1)
    sa = jnp.concatenate([-sin, zh, zeros], axis=1)
    sb = jnp.concatenate([zh, sin, zeros], axis=1)
    return tuple(jnp.tile(t, (1, 128 // NSA_DH)) for t in (c, sa, sb))


def _ret_rope_tables(pos):
    half = RET_DK // 2
    cos, sin = _rope_angles(pos, half, RET_THETA)
    return jnp.concatenate([cos, cos], axis=1), jnp.concatenate([-sin, sin], axis=1)


def _retention_decay_tables():
    H, C = RET_HEADS, RET_CHUNK
    log_g = np.log1p(-np.exp2(-5.0 - np.arange(H))).astype(np.float32)
    n = np.arange(C, dtype=np.float32)
    diff = n[:, None] - n[None, :]
    intra = np.where(diff >= 0, np.exp(np.maximum(diff, 0.0)[None] * log_g[:, None, None]), 0.0).astype(np.float32)
    q_dec = np.exp((n[None, :] + 1.0) * log_g[:, None]).astype(np.float32)
    k_dec = np.exp((C - 1.0 - n[None, :]) * log_g[:, None]).astype(np.float32)
    c_dec = np.exp(C * log_g).astype(np.float32)
    qd = np.broadcast_to(q_dec[:, :, None], (H, C, RET_DK))
    kd = np.broadcast_to(k_dec[:, :, None], (H, C, RET_DK))
    cd = np.broadcast_to(c_dec[:, None, None], (H, 8, RET_DK))
    return tuple(jnp.asarray(np.ascontiguousarray(t)) for t in (intra, qd, kd, cd))


def _overlap_t():
    cs0 = np.arange(N_CMP_PAD) * CMP_STRIDE
    ss0 = np.arange(N_SEL) * SEL_BLOCK
    ov = np.clip(np.minimum(cs0[:, None] + CMP_BLOCK, ss0[None, :] + SEL_BLOCK)
                 - np.maximum(cs0[:, None], ss0[None, :]), 0, None)
    ov = (ov / CMP_BLOCK).astype(np.float32)
    ov[N_CMP:, :] = 0.0
    return jnp.asarray(np.ascontiguousarray(ov.T))


def _block_onehot():
    e = np.zeros((SEQ, NSA_DH), np.float32)
    e[np.arange(SEQ), np.arange(SEQ) // SEL_BLOCK] = 1.0
    return jnp.asarray(e, dtype=BF16)


def _layer(x2, batch, p, tabs, final_g, final_norm):
    n = x2.shape[0]
    w_in = p["w_in"]
    w_main = jnp.concatenate(
        [w_in[:, :NG_OFF + NG_COLS], jnp.zeros((D_MODEL, NG_PAD - NG_COLS), w_in.dtype),
         w_in[:, NG_OFF + NG_COLS:GATE_OFF_ORIG]], axis=1).astype(BF16)
    w_gate = w_in[:, GATE_OFF_ORIG:].astype(BF16)

    ya, q, kvc, kv2, ng, rq, rg, yd = _inproj(
        x2, p["norm1_g"][None, :], w_main, p["sc_conv"], tabs["tok"], p["pool_w"].astype(BF16),
        p["pool_scale"][None, :])

    xc = kvc.reshape(batch, N_CMP_PAD, CMP_STRIDE, 2 * NSA_GROUPS, NSA_DH).transpose(0, 3, 1, 2, 4)
    xc = xc.reshape(batch, 2 * NSA_GROUPS, N_CMP_PAD, CMP_STRIDE * NSA_DH)
    pe = p["cmp_pe"]
    pea = pe[:CMP_STRIDE].reshape(1, -1)
    peb = pe[CMP_STRIDE:].reshape(1, -1)
    w1 = jnp.stack([p["cmp_w1_k"], p["cmp_w1_v"]])
    w1a = w1[:, :CMP_STRIDE].reshape(2, CMP_STRIDE * NSA_DH, NSA_DH).astype(BF16)
    w1b = w1[:, CMP_STRIDE:].reshape(2, CMP_STRIDE * NSA_DH, NSA_DH).astype(BF16)
    w2 = jnp.stack([p["cmp_w2_k"], p["cmp_w2_v"]])
    w2p = jnp.concatenate([w2, jnp.zeros_like(w2)], axis=2).astype(BF16)
    kcvc = _compress(xc, pea, peb, w1a, w1b, w2p, tabs["cmp"])

    q3 = q.reshape(batch, SEQ, BRANCH)
    ocmp, selbias = _cmp_attn(q3, kcvc, tabs["ovt"])
    yb = _sel_win(q3, kv2.reshape(batch, SEQ, 512), selbias, tabs["onehot"], ocmp, ng.reshape(batch, SEQ, NG_PAD))

    yc = _retention(rq.reshape(batch, SEQ, 3 * BRANCH), rg.reshape(batch, SEQ, BRANCH), *tabs["ret"],
                    p["ret_gn_g"][None, :])

    x2 = _merge(x2, p["norm1_g"][None, :], w_gate, p["b_gate"][None, :], ya, yb.reshape(n, BRANCH),
                yc.reshape(n, BRANCH), yd, p["w_branch"].astype(BF16), p["w_o"].astype(BF16))
    return _mlp(x2, p["norm2_g"][None, :], p["w_up"].astype(BF16), p["ffn_conv"], p["w_down"].astype(BF16),
                final_g[None, :], final_norm)


def kernel(x, norm1_g, w_in, b_gate, sc_conv, cmp_pe, cmp_w1_k, cmp_w2_k, cmp_w1_v, cmp_w2_v, ret_gn_g, pool_w, pool_scale, w_branch, w_o, norm2_g, w_up, ffn_conv, w_down, final_norm_g):
    batch, seq, d = x.shape
    assert seq == SEQ and d == D_MODEL
    depth = w_in.shape[0]
    params = dict(norm1_g=norm1_g, w_in=w_in, b_gate=b_gate, sc_conv=sc_conv, cmp_pe=cmp_pe, cmp_w1_k=cmp_w1_k,
                  cmp_w2_k=cmp_w2_k, cmp_w1_v=cmp_w1_v, cmp_w2_v=cmp_w2_v, ret_gn_g=ret_gn_g, pool_w=pool_w,
                  pool_scale=pool_scale, w_branch=w_branch, w_o=w_o, norm2_g=norm2_g, w_up=w_up, ffn_conv=ffn_conv,
                  w_down=w_down)
    pos = jnp.arange(SEQ)
    cmp_end = jnp.asarray(np.arange(N_CMP_PAD) * CMP_STRIDE + CMP_BLOCK - 1)
    tabs = dict(tok=_nsa_rope_tables(pos) + _ret_rope_tables(pos), cmp=_nsa_rope_tables(cmp_end),
                ret=_retention_decay_tables(), ovt=_overlap_t(), onehot=_block_onehot())
    x2 = x.reshape(batch * seq, d)
    for l in range(depth):
        p = {k: v[l] for k, v in params.items()}
        x2 = _layer(x2, batch, p, tabs, final_norm_g, final_norm=(l == depth - 1))
    return x2.reshape(batch, seq, d)
```

```python
import functools

import numpy as np
import jax
import jax.numpy as jnp
from jax import lax
from jax.experimental import pallas as pl
from jax.experimental.pallas import tpu as pltpu

F32 = jnp.float32
BF16 = jnp.bfloat16

D_MODEL = 1024
SEQ = 2048
BRANCH = 512
N_BRANCH = 4
CONV_TAPS = 3
NSA_HEADS = 8
NSA_GROUPS = 2
NSA_HPG = NSA_HEADS // NSA_GROUPS
NSA_DH = BRANCH // NSA_HEADS
CMP_BLOCK = 32
CMP_STRIDE = 16
N_CMP = (SEQ - CMP_BLOCK) // CMP_STRIDE + 1
N_CMP_PAD = SEQ // CMP_STRIDE
SEL_BLOCK = 64
N_SEL = SEQ // SEL_BLOCK
SEL_TOP_N = 16
WINDOW = 512
ROPE_THETA = 500000.0
ROPE_DIM = NSA_DH // 4
FORCE_SCORE = 1.0e4
RET_HEADS = 4
RET_DK = BRANCH // RET_HEADS
RET_CHUNK = 128
RET_THETA = 10000.0
POOL_WINDOWS = (2, 4, 8, 16)
POOL_GROUP_DIM = BRANCH // len(POOL_WINDOWS)
D_FF = 2816
EPS = 1e-6
NEG_INF = -1e30
LOG2_E = 1.4426950408889634
Q_SCALE = NSA_DH ** -0.5 * LOG2_E

SC_OFF = 0
Q_OFF = 3 * BRANCH
KV_OFF = Q_OFF + BRANCH
NG_OFF = KV_OFF + 3 * 2 * NSA_GROUPS * NSA_DH
NG_COLS = 3 * NSA_HEADS
NG_PAD = 128
RET_OFF = NG_OFF + NG_PAD
POOL_OFF = RET_OFF + 4 * BRANCH
MAIN_COLS = POOL_OFF + BRANCH
GATE_OFF_ORIG = NG_OFF + NG_COLS + 4 * BRANCH + BRANCH

TOK_TILE = 512
TILES_PER_SEQ = SEQ // TOK_TILE
CONV_HALO = 8
POOL_HALO = 16
ATT_TQ = 256
ATT_TK = 256
WIN_SPAN = WINDOW + ATT_TQ
KV_LANES = 4 * NSA_GROUPS * 128
CMP_TQ = 512
VMEM_LIMIT = 56 * 1024 * 1024


def _const_spec(shape):
    n = len(shape)
    return pl.BlockSpec(shape, lambda *_: (0,) * n, pipeline_mode=pl.Buffered(1))


def _rms(x, g):
    ms = jnp.mean(x * x, axis=-1, keepdims=True)
    return x * lax.rsqrt(ms + EPS) * g


def _dot(a, b):
    return jnp.dot(a, b, preferred_element_type=F32)


def _dot_nt(a, b):
    return lax.dot_general(a, b, (((1,), (1,)), ((), ())), preferred_element_type=F32)


def _inproj_kernel(x_ref, g_ref, w_ref, scw_ref, nc_ref, nsa_ref, nsb_ref, rc_ref, rs_ref, oh_ref, pw_ref, ps_ref,
                   ya_ref, q_ref, kvc_ref, kv2_ref, ng_ref, rq_ref, rg_ref, yd_ref, ch_buf, u_buf):
    T = TOK_TILE
    i = pl.program_id(0)

    @pl.when(i % TILES_PER_SEQ == 0)
    def _():
        ch_buf[0:CONV_HALO, :] = jnp.zeros((CONV_HALO, BRANCH), F32)
        u_buf[0:POOL_HALO, :] = jnp.zeros((POOL_HALO, BRANCH), F32)

    h = _rms(x_ref[...], g_ref[...]).astype(BF16)

    def proj(off, width):
        return _dot(h, w_ref[:, off:off + width])

    zb = proj(SC_OFF, BRANCH)
    ch_buf[CONV_HALO:CONV_HALO + T, :] = proj(SC_OFF + BRANCH, BRANCH) * proj(SC_OFF + 2 * BRANCH, BRANCH)
    conv = scw_ref[CONV_TAPS - 1:CONV_TAPS, :] * ch_buf[CONV_HALO:CONV_HALO + T, :]
    for k in range(1, CONV_TAPS):
        conv = conv + scw_ref[CONV_TAPS - 1 - k:CONV_TAPS - k, :] * ch_buf[CONV_HALO - k:CONV_HALO - k + T, :]
    ya_ref[...] = (zb * conv).astype(BF16)
    ch_buf[0:CONV_HALO, :] = ch_buf[T:T + CONV_HALO, :]

    nc, nsa, nsb = nc_ref[...], nsa_ref[...], nsb_ref[...]

    def rope_nsa(x):
        return x * nc + pltpu.roll(x, 128 - ROPE_DIM // 2, 1) * nsa + pltpu.roll(x, ROPE_DIM // 2, 1) * nsb

    zq = proj(Q_OFF, BRANCH)
    for c in range(BRANCH // 128):
        q_ref[:, c * 128:(c + 1) * 128] = (rope_nsa(zq[:, c * 128:(c + 1) * 128]) * Q_SCALE).astype(BF16)
    zkv = proj(KV_OFF, 768)
    kvc_ref[...] = zkv[:, 0:256]
    pieces = (rope_nsa(zkv[:, 256:384]).astype(BF16), zkv[:, 384:512].astype(BF16),
              rope_nsa(zkv[:, 512:640]).astype(BF16), zkv[:, 640:768].astype(BF16))
    fills = (oh_ref[...], jnp.ones((T, NSA_DH), BF16), jnp.zeros((T, NSA_DH), BF16), jnp.ones((T, NSA_DH), BF16))
    for c, (piece, fill) in enumerate(zip(pieces, fills)):
        for g in range(NSA_GROUPS):
            kv2_ref[:, (2 * c + g) * 128:(2 * c + g + 1) * 128] = jnp.concatenate(
                [piece[:, g * NSA_DH:(g + 1) * NSA_DH], fill], axis=1)
    ng_ref[...] = jax.nn.sigmoid(proj(NG_OFF, NG_PAD))

    rc, rs = rc_ref[...], rs_ref[...]

    def rope_ret(x):
        return x * rc + pltpu.roll(x, RET_DK // 2, 1) * rs

    zr = proj(RET_OFF, BRANCH)
    for c in range(RET_HEADS):
        rq_ref[:, c * 128:(c + 1) * 128] = rope_ret(zr[:, c * 128:(c + 1) * 128]).astype(BF16)
    zr = proj(RET_OFF + BRANCH, BRANCH)
    for c in range(RET_HEADS):
        rq_ref[:, BRANCH + c * 128:BRANCH + (c + 1) * 128] = (
            rope_ret(zr[:, c * 128:(c + 1) * 128]) * (RET_DK ** -0.5)).astype(BF16)
    rq_ref[:, 2 * BRANCH:3 * BRANCH] = proj(RET_OFF + 2 * BRANCH, BRANCH).astype(BF16)
    rg_ref[...] = proj(RET_OFF + 3 * BRANCH, BRANCH)

    zu = proj(POOL_OFF, BRANCH)
    u_buf[POOL_HALO:POOL_HALO + T, :] = zu
    pos = (i % TILES_PER_SEQ) * T + lax.broadcasted_iota(jnp.int32, (T, 1), 0)
    for gi, win in enumerate(POOL_WINDOWS):
        c0, c1 = gi * POOL_GROUP_DIM, (gi + 1) * POOL_GROUP_DIM
        s = zu[:, c0:c1]
        for k in range(1, win):
            s = s + u_buf[POOL_HALO - k:POOL_HALO - k + T, c0:c1]
        cnt = jnp.minimum(pos + 1, win).astype(F32)
        pooled = s / cnt - zu[:, c0:c1]
        y = _dot(pooled.astype(BF16), pw_ref[gi]) * ps_ref[:, c0:c1]
        yd_ref[:, c0:c1] = y.astype(BF16)
    u_buf[0:POOL_HALO, :] = u_buf[T:T + POOL_HALO, :]


def _inproj(x2, g, w_main, scw, tabs, onehot, pool_w, pool_scale):
    n = x2.shape[0]
    T = TOK_TILE
    tok = lambda width: pl.BlockSpec((T, width), lambda i: (i, 0))
    tab = pl.BlockSpec((T, 128), lambda i: (i % TILES_PER_SEQ, 0))
    out_shapes = (
        jax.ShapeDtypeStruct((n, BRANCH), BF16),
        jax.ShapeDtypeStruct((n, BRANCH), BF16),
        jax.ShapeDtypeStruct((n, 256), F32),
        jax.ShapeDtypeStruct((n, KV_LANES), BF16),
        jax.ShapeDtypeStruct((n, NG_PAD), F32),
        jax.ShapeDtypeStruct((n, 3 * BRANCH), BF16),
        jax.ShapeDtypeStruct((n, BRANCH), F32),
        jax.ShapeDtypeStruct((n, BRANCH), BF16),
    )
    return pl.pallas_call(
        _inproj_kernel,
        name="inproj",
        grid=(n // T,),
        in_specs=[tok(D_MODEL), _const_spec((1, D_MODEL)), _const_spec((D_MODEL, MAIN_COLS)),
                  _const_spec((CONV_TAPS, BRANCH)), tab, tab, tab, tab, tab,
                  pl.BlockSpec((T, NSA_DH), lambda i: (i % TILES_PER_SEQ, 0)),
                  _const_spec((len(POOL_WINDOWS), POOL_GROUP_DIM, POOL_GROUP_DIM)), _const_spec((1, BRANCH))],
        out_specs=[tok(BRANCH), tok(BRANCH), tok(256), tok(KV_LANES), tok(NG_PAD), tok(3 * BRANCH), tok(BRANCH),
                   tok(BRANCH)],
        out_shape=out_shapes,
        scratch_shapes=[pltpu.VMEM((CONV_HALO + T, BRANCH), F32), pltpu.VMEM((POOL_HALO + T, BRANCH), F32)],
        compiler_params=pltpu.CompilerParams(dimension_semantics=("arbitrary",), vmem_limit_bytes=VMEM_LIMIT),
    )(x2, g, w_main, scw, *tabs, onehot, pool_w, pool_scale)


def _compress_kernel(x_ref, pea_ref, peb_ref, w1a_ref, w1b_ref, w2_ref, cc_ref, csa_ref, csb_ref, out_ref):
    pea, peb = pea_ref[...], peb_ref[...]
    for idx in range(2 * NSA_GROUPS):
        kv = idx // NSA_GROUPS
        x = x_ref[0, idx]
        first = _dot((x + pea).astype(BF16), w1a_ref[kv])
        second = _dot((x + peb).astype(BF16), w1b_ref[kv])
        hid = jax.nn.gelu(first + pltpu.roll(second, N_CMP_PAD - 1, 0))
        y = _dot(hid.astype(BF16), w2_ref[kv])
        if kv == 0:
            y = (y * cc_ref[...] + pltpu.roll(y, 128 - ROPE_DIM // 2, 1) * csa_ref[...]
                 + pltpu.roll(y, ROPE_DIM // 2, 1) * csb_ref[...])
        out_ref[0, idx] = y[:, 0:NSA_DH].astype(BF16)


def _compress(xc, pea, peb, w1a, w1b, w2p, ctabs):
    b = xc.shape[0]
    return pl.pallas_call(
        _compress_kernel,
        name="compress",
        grid=(b,),
        in_specs=[pl.BlockSpec((1, 4, N_CMP_PAD, 1024), lambda i: (i, 0, 0, 0)),
                  _const_spec((1, 1024)), _const_spec((1, 1024)),
                  _const_spec((2, 1024, NSA_DH)), _const_spec((2, 1024, NSA_DH)), _const_spec((2, NSA_DH, 128)),
                  _const_spec((N_CMP_PAD, 128)), _const_spec((N_CMP_PAD, 128)), _const_spec((N_CMP_PAD, 128))],
        out_specs=pl.BlockSpec((1, 4, N_CMP_PAD, NSA_DH), lambda i: (i, 0, 0, 0)),
        out_shape=jax.ShapeDtypeStruct((b, 4, N_CMP_PAD, NSA_DH), BF16),
        compiler_params=pltpu.CompilerParams(dimension_semantics=("arbitrary",), vmem_limit_bytes=VMEM_LIMIT),
    )(xc, pea, peb, w1a, w1b, w2p, *ctabs)


def _cmp_attn_kernel(q_ref, kc_ref, vc_ref, ovt_ref, o_ref, sb_ref):
    R = CMP_TQ
    r = pl.program_id(2)
    kc, vc = kc_ref[0, 0], vc_ref[0, 0]
    t_rows = r * R + lax.broadcasted_iota(jnp.int32, (R, N_CMP_PAD), 0)
    blk_end = lax.broadcasted_iota(jnp.int32, (R, N_CMP_PAD), 1) * CMP_STRIDE + (CMP_BLOCK - 1)
    valid = blk_end <= t_rows
    qs = jnp.concatenate([q_ref[0, :, j * NSA_DH:(j + 1) * NSA_DH] for j in range(NSA_HPG)], axis=0)
    sm = jnp.where(valid[None], _dot_nt(qs, kc).reshape(NSA_HPG, R, N_CMP_PAD), NEG_INF)
    e = jnp.exp2(sm - jnp.max(sm, axis=-1, keepdims=True))
    p = jnp.where(valid[None], e, 0.0) / jnp.sum(e, axis=-1, keepdims=True)
    o = _dot(p.reshape(NSA_HPG * R, N_CMP_PAD).astype(BF16), vc)
    for j in range(NSA_HPG):
        o_ref[0, :, j * NSA_DH:(j + 1) * NSA_DH] = o[j * R:(j + 1) * R]
    psum = jnp.sum(p, axis=0)
    imp = lax.dot_general(ovt_ref[...], psum, (((1,), (1,)), ((), ())), preferred_element_type=F32,
                          precision=lax.Precision.HIGHEST)
    cur = (r * R + lax.broadcasted_iota(jnp.int32, (N_SEL, R), 1)) // SEL_BLOCK
    bid = lax.broadcasted_iota(jnp.int32, (N_SEL, R), 0)
    forced = (bid == 0) | (bid == cur) | (bid == cur - 1)
    imp = jnp.where(forced, FORCE_SCORE, jnp.where(bid > cur, -FORCE_SCORE, imp))
    rank = jnp.zeros((N_SEL, R), F32)
    for m in range(N_SEL):
        a = imp[m:m + 1, :]
        before = jnp.where(bid > m, jnp.where(a >= imp, 1.0, 0.0), jnp.where(a > imp, 1.0, 0.0))
        rank = rank + before
    bias_t = jnp.where(rank < SEL_TOP_N, 0.0, NEG_INF)
    bias = bias_t.T
    sb_ref[0, 0] = jnp.concatenate([bias, jnp.zeros((R, NSA_DH - N_SEL), F32)], axis=1).astype(BF16)


def _cmp_attn(q3, kcvc, ovt):
    b = q3.shape[0]
    R = CMP_TQ
    gw = NSA_HPG * NSA_DH
    return pl.pallas_call(
        _cmp_attn_kernel,
        name="cmp_attn",
        grid=(b, NSA_GROUPS, SEQ // R),
        in_specs=[pl.BlockSpec((1, R, gw), lambda i, g, r: (i, r, g)),
                  pl.BlockSpec((1, 1, N_CMP_PAD, NSA_DH), lambda i, g, r: (i, g, 0, 0)),
                  pl.BlockSpec((1, 1, N_CMP_PAD, NSA_DH), lambda i, g, r: (i, NSA_GROUPS + g, 0, 0)),
                  _const_spec((N_SEL, N_CMP_PAD))],
        out_specs=[pl.BlockSpec((1, R, gw), lambda i, g, r: (i, r, g)),
                   pl.BlockSpec((1, 1, R, NSA_DH), lambda i, g, r: (i, g, r, 0))],
        out_shape=(jax.ShapeDtypeStruct((b, SEQ, BRANCH), F32),
                   jax.ShapeDtypeStruct((b, NSA_GROUPS, SEQ, NSA_DH), BF16)),
        compiler_params=pltpu.CompilerParams(dimension_semantics=("arbitrary",) * 3, vmem_limit_bytes=VMEM_LIMIT),
    )(q3, kcvc, kcvc, ovt)


def _sel_win_kernel(q_ref, kv_ref, sb_ref, osel_ref, owin_ref, qa_ref, m_ref, acc_ref):
    TQ, TK = ATT_TQ, ATT_TK
    M = NSA_HPG * TQ
    qi = pl.program_id(1)

    def chunk(rows, c, g):
        return kv_ref[0, rows, (2 * c + g) * 128:(2 * c + g + 1) * 128]

    for g in range(NSA_GROUPS):
        sb = sb_ref[0, g]
        for j in range(NSA_HPG):
            h = g * NSA_HPG + j
            qa_ref[g, j * TQ:(j + 1) * TQ, :] = jnp.concatenate([q_ref[0, :, h * NSA_DH:(h + 1) * NSA_DH], sb], axis=1)

    m_ref[...] = jnp.full(m_ref.shape, NEG_INF, F32)
    acc_ref[...] = jnp.zeros(acc_ref.shape, F32)

    def online(idx, s, v):
        m = m_ref[idx]
        m_new = jnp.maximum(m, jnp.max(s, axis=-1, keepdims=True))
        p = jnp.exp2(s - pltpu.repeat(m_new, TK // 128, axis=1))
        acc_ref[idx] = jnp.exp2(m - m_new) * acc_ref[idx] + _dot(p.astype(BF16), v)
        m_ref[idx] = m_new

    def tile(t, carry, near):
        rows = pl.ds(pl.multiple_of(t * TK, TK), TK)
        if near:
            dist = (lax.broadcasted_iota(jnp.int32, (TQ, TK), 0) - lax.broadcasted_iota(jnp.int32, (TQ, TK), 1)
                    + (qi - t) * TK)
            causal_bias = jnp.where(dist >= 0, 0.0, NEG_INF)
            window_bias = jnp.where(lax.bitcast_convert_type(dist, jnp.uint32) < WINDOW, 0.0, NEG_INF)

        def masked(s, bias):
            return (s.reshape(NSA_HPG, TQ, TK) + bias[None]).reshape(M, TK)

        for g in range(NSA_GROUPS):
            qa = qa_ref[g]
            s = _dot_nt(qa, chunk(rows, 0, g))
            online(g, masked(s, causal_bias) if near else s, chunk(rows, 1, g))
            if near:
                online(NSA_GROUPS + g, masked(_dot_nt(qa, chunk(rows, 2, g)), window_bias), chunk(rows, 3, g))
        return carry

    n_far = jnp.maximum(qi - WINDOW // TK, 0)
    lax.fori_loop(0, n_far, functools.partial(tile, near=False), 0)
    lax.fori_loop(n_far, qi + 1, functools.partial(tile, near=True), 0)

    lane = lax.broadcasted_iota(jnp.int32, (TQ, 128), 1)
    for idx, o_ref in ((0, osel_ref), (NSA_GROUPS, owin_ref)):
        for g in range(NSA_GROUPS):
            halves = []
            for j in range(NSA_HPG):
                acc = acc_ref[idx + g, j * TQ:(j + 1) * TQ, :]
                rolled = pltpu.roll(acc, NSA_DH, 1)
                halves.append(acc * (1.0 / rolled) if j % 2 == 0 else rolled * (1.0 / acc))
            for c in range(NSA_HPG // 2):
                o_ref[0, :, (g * 2 + c) * 128:(g * 2 + c + 1) * 128] = jnp.where(
                    lane < NSA_DH, halves[2 * c], halves[2 * c + 1])


def _sel_win(q3, kv3, selbias):
    b = q3.shape[0]
    TQ = ATT_TQ
    ospec = pl.BlockSpec((1, TQ, BRANCH), lambda i, t: (i, t, 0))
    return pl.pallas_call(
        _sel_win_kernel,
        name="sel_win_attn",
        grid=(b, SEQ // TQ),
        in_specs=[pl.BlockSpec((1, TQ, BRANCH), lambda i, t: (i, t, 0)),
                  pl.BlockSpec((1, SEQ, KV_LANES), lambda i, t: (i, 0, 0)),
                  pl.BlockSpec((1, NSA_GROUPS, TQ, NSA_DH), lambda i, t: (i, 0, t, 0))],
        out_specs=[ospec, ospec],
        out_shape=(jax.ShapeDtypeStruct((b, SEQ, BRANCH), F32), jax.ShapeDtypeStruct((b, SEQ, BRANCH), F32)),
        scratch_shapes=[pltpu.VMEM((NSA_GROUPS, NSA_HPG * TQ, 128), BF16),
                        pltpu.VMEM((2 * NSA_GROUPS, NSA_HPG * TQ, 128), F32),
                        pltpu.VMEM((2 * NSA_GROUPS, NSA_HPG * TQ, 128), F32)],
        compiler_params=pltpu.CompilerParams(dimension_semantics=("arbitrary", "arbitrary"),
                                             vmem_limit_bytes=VMEM_LIMIT),
    )(q3, kv3, selbias)


def _retention_kernel(q_ref, k_ref, v_ref, zg_ref, intra_ref, qd_ref, kd_ref, cd_ref, gn_ref, y_ref):
    C = RET_CHUNK
    intra, qd, kd = intra_ref[0], qd_ref[0], kd_ref[0]
    cd = cd_ref[0, 0:1, :]
    gn = gn_ref[...]
    state = jnp.zeros((RET_DK, RET_DK), F32)
    for c in range(SEQ // C):
        rows = slice(c * C, (c + 1) * C)
        qc, kc, vc = q_ref[0, rows, :], k_ref[0, rows, :], v_ref[0, rows, :]
        scores = _dot_nt(qc, kc) * intra
        inner = _dot(scores.astype(BF16), vc)
        cross = _dot((qc.astype(F32) * qd).astype(BF16), state.astype(BF16))
        kdec = (kc.astype(F32) * kd).astype(BF16)
        state = state * cd + lax.dot_general(kdec, vc, (((0,), (0,)), ((), ())), preferred_element_type=F32)
        o = inner + cross
        mu = jnp.mean(o, axis=-1, keepdims=True)
        d = o - mu
        var = jnp.mean(d * d, axis=-1, keepdims=True)
        o = d * lax.rsqrt(var + EPS) * gn
        y_ref[0, rows, :] = (o * jax.nn.silu(zg_ref[0, rows, :])).astype(BF16)


def _retention(rq3, rg3, intra, qd, kd, cd, gn):
    b = rq3.shape[0]
    hspec = lambda off: pl.BlockSpec((1, SEQ, RET_DK), lambda i, h: (i, 0, off + h))
    tspec = pl.BlockSpec((1, RET_CHUNK, RET_DK), lambda i, h: (h, 0, 0))
    return pl.pallas_call(
        _retention_kernel,
        name="retention",
        grid=(b, RET_HEADS),
        in_specs=[hspec(0), hspec(RET_HEADS), hspec(2 * RET_HEADS), hspec(0), tspec, tspec, tspec,
                  pl.BlockSpec((1, 8, RET_DK), lambda i, h: (h, 0, 0)),
                  pl.BlockSpec((1, RET_DK), lambda i, h: (0, h))],
        out_specs=hspec(0),
        out_shape=jax.ShapeDtypeStruct((b, SEQ, BRANCH), BF16),
        compiler_params=pltpu.CompilerParams(dimension_semantics=("arbitrary", "arbitrary"),
                                             vmem_limit_bytes=VMEM_LIMIT),
    )(rq3, rq3, rq3, rg3, intra, qd, kd, cd, gn)


def _merge_kernel(x_ref, g_ref, wg_ref, bg_ref, ya_ref, oc_ref, os_ref, ow_ref, ng_ref, yc_ref, yd_ref, wb_ref, wo_ref,
                  o_ref):
    T = TOK_TILE
    x = x_ref[...]
    h = _rms(x, g_ref[...]).astype(BF16)
    ng = ng_ref[...]
    yb = None
    for br, o_br in enumerate((oc_ref, os_ref, ow_ref)):
        gate = jnp.concatenate([jnp.broadcast_to(ng[:, 3 * hd + br:3 * hd + br + 1], (T, NSA_DH))
                                for hd in range(NSA_HEADS)], axis=1)
        yb = gate * o_br[...] if yb is None else yb + gate * o_br[...]
    ys = (ya_ref[...], yb.astype(BF16), yc_ref[...], yd_ref[...])
    merged = None
    for n, y in enumerate(ys):
        gate = jax.nn.sigmoid(_dot(h, wg_ref[:, n * D_MODEL:(n + 1) * D_MODEL])
                              + bg_ref[:, n * D_MODEL:(n + 1) * D_MODEL])
        term = _dot(y, wb_ref[n]) * gate
        merged = term if merged is None else merged + term
    o_ref[...] = x + _dot(merged.astype(BF16), wo_ref[...])


def _merge(x2, g, wg, bg, ya, oc, osel, owin, ng, yc, yd, wb, wo):
    n = x2.shape[0]
    T = TOK_TILE
    tok = lambda width: pl.BlockSpec((T, width), lambda i: (i, 0))
    return pl.pallas_call(
        _merge_kernel,
        name="merge",
        grid=(n // T,),
        in_specs=[tok(D_MODEL), _const_spec((1, D_MODEL)), _const_spec((D_MODEL, N_BRANCH * D_MODEL)),
                  _const_spec((1, N_BRANCH * D_MODEL)), tok(BRANCH), tok(BRANCH), tok(BRANCH), tok(BRANCH), tok(NG_PAD),
                  tok(BRANCH), tok(BRANCH),
                  _const_spec((N_BRANCH, BRANCH, D_MODEL)), _const_spec((D_MODEL, D_MODEL))],
        out_specs=tok(D_MODEL),
        out_shape=jax.ShapeDtypeStruct((n, D_MODEL), F32),
        compiler_params=pltpu.CompilerParams(dimension_semantics=("arbitrary",), vmem_limit_bytes=VMEM_LIMIT),
    )(x2, g, wg, bg, ya, oc, osel, owin, ng, yc, yd, wb, wo)


FF_CHUNK = 256


def _mlp_kernel(x_ref, g_ref, wup_ref, cw_ref, wdn_ref, fg_ref, o_ref, ug_buf, act_buf, *, final_norm):
    T = TOK_TILE
    i = pl.program_id(0)

    @pl.when(i % TILES_PER_SEQ == 0)
    def _():
        ug_buf[0:CONV_HALO, :] = jnp.zeros((CONV_HALO, D_FF), F32)

    x = x_ref[...]
    h = _rms(x, g_ref[...]).astype(BF16)
    for c in range(D_FF // FF_CHUNK):
        cols = slice(c * FF_CHUNK, (c + 1) * FF_CHUNK)
        ug_buf[CONV_HALO:CONV_HALO + T, cols] = _dot(h, wup_ref[:, cols])
        conv = cw_ref[CONV_TAPS - 1:CONV_TAPS, cols] * ug_buf[CONV_HALO:CONV_HALO + T, cols]
        for k in range(1, CONV_TAPS):
            conv = conv + cw_ref[CONV_TAPS - 1 - k:CONV_TAPS - k, cols] * ug_buf[CONV_HALO - k:CONV_HALO - k + T, cols]
        val = _dot(h, wup_ref[:, D_FF + c * FF_CHUNK:D_FF + (c + 1) * FF_CHUNK])
        act_buf[:, cols] = (jax.nn.silu(conv) * val).astype(BF16)
    ug_buf[0:CONV_HALO, :] = ug_buf[T:T + CONV_HALO, :]
    y = x + _dot(act_buf[...], wdn_ref[...])
    if final_norm:
        y = _rms(y, fg_ref[...])
    o_ref[...] = y


def _mlp(x2, g, wup, cw, wdn, fg, final_norm):
    n = x2.shape[0]
    T = TOK_TILE
    tok = pl.BlockSpec((T, D_MODEL), lambda i: (i, 0))
    return pl.pallas_call(
        functools.partial(_mlp_kernel, final_norm=final_norm),
        name="mlp",
        grid=(n // T,),
        in_specs=[tok, _const_spec((1, D_MODEL)), _const_spec((D_MODEL, 2 * D_FF)), _const_spec((CONV_TAPS, D_FF)),
                  _const_spec((D_FF, D_MODEL)), _const_spec((1, D_MODEL))],
        out_specs=tok,
        out_shape=jax.ShapeDtypeStruct((n, D_MODEL), F32),
        scratch_shapes=[pltpu.VMEM((CONV_HALO + T, D_FF), F32), pltpu.VMEM((T, D_FF), BF16)],
        compiler_params=pltpu.CompilerParams(dimension_semantics=("arbitrary",), vmem_limit_bytes=VMEM_LIMIT),
    )(x2, g, wup, cw, wdn, fg)


def _rope_angles(pos, half, theta):
    inv_freq = np.power(np.float32(theta), -np.arange(half, dtype=np.float32) / np.float32(half))
    ang = pos.astype(F32)[:, None] * jnp.asarray(inv_freq)[None, :]
    return jnp.cos(ang), jnp.sin(ang)


def _nsa_rope_tables(pos):
    half = ROPE_DIM // 2
    cos, sin = _rope_angles(pos, half, ROPE_THETA)
    n = pos.shape[0]
    ones = jnp.ones((n, NSA_DH - ROPE_DIM), F32)
    zeros = jnp.zeros((n, NSA_DH - ROPE_DIM), F32)
    zh = jnp.zeros((n, half), F32)
    c = jnp.concatenate([cos, cos, ones], axis=1)
    sa = jnp.concatenate([-sin, zh, zeros], axis=1)
    sb = jnp.concatenate([zh, sin, zeros], axis=1)
    return tuple(jnp.tile(t, (1, 128 // NSA_DH)) for t in (c, sa, sb))


def _ret_rope_tables(pos):
    half = RET_DK // 2
    cos, sin = _rope_angles(pos, half, RET_THETA)
    return jnp.concatenate([cos, cos], axis=1), jnp.concatenate([-sin, sin], axis=1)


def _retention_decay_tables():
    H, C = RET_HEADS, RET_CHUNK
    log_g = np.log1p(-np.exp2(-5.0 - np.arange(H))).astype(np.float32)
    n = np.arange(C, dtype=np.float32)
    diff = n[:, None] - n[None, :]
    intra = np.where(diff >= 0, np.exp(np.maximum(diff, 0.0)[None] * log_g[:, None, None]), 0.0).astype(np.float32)
    q_dec = np.exp((n[None, :] + 1.0) * log_g[:, None]).astype(np.float32)
    k_dec = np.exp((C - 1.0 - n[None, :]) * log_g[:, None]).astype(np.float32)
    c_dec = np.exp(C * log_g).astype(np.float32)
    qd = np.broadcast_to(q_dec[:, :, None], (H, C, RET_DK))
    kd = np.broadcast_to(k_dec[:, :, None], (H, C, RET_DK))
    cd = np.broadcast_to(c_dec[:, None, None], (H, 8, RET_DK))
    return tuple(jnp.asarray(np.ascontiguousarray(t)) for t in (intra, qd, kd, cd))


def _overlap_t():
    cs0 = np.arange(N_CMP_PAD) * CMP_STRIDE
    ss0 = np.arange(N_SEL) * SEL_BLOCK
    ov = np.clip(np.minimum(cs0[:, None] + CMP_BLOCK, ss0[None, :] + SEL_BLOCK)
                 - np.maximum(cs0[:, None], ss0[None, :]), 0, None)
    ov = (ov / CMP_BLOCK).astype(np.float32)
    ov[N_CMP:, :] = 0.0
    return jnp.asarray(np.ascontiguousarray(ov.T))


def _block_onehot():
    e = np.zeros((SEQ, NSA_DH), np.float32)
    e[np.arange(SEQ), np.arange(SEQ) // SEL_BLOCK] = 1.0
    return jnp.asarray(e, dtype=BF16)


def _layer(x2, batch, p, tabs, final_g, final_norm):
    n = x2.shape[0]
    w_in = p["w_in"]
    w_main = jnp.concatenate(
        [w_in[:, :NG_OFF + NG_COLS], jnp.zeros((D_MODEL, NG_PAD - NG_COLS), w_in.dtype),
         w_in[:, NG_OFF + NG_COLS:GATE_OFF_ORIG]], axis=1).astype(BF16)
    w_gate = w_in[:, GATE_OFF_ORIG:].astype(BF16)

    ya, q, kvc, kv2, ng, rq, rg, yd = _inproj(
        x2, p["norm1_g"][None, :], w_main, p["sc_conv"], tabs["tok"], tabs["onehot"], p["pool_w"].astype(BF16),
        p["pool_scale"][None, :])

    xc = kvc.reshape(batch, N_CMP_PAD, CMP_STRIDE, 2 * NSA_GROUPS, NSA_DH).transpose(0, 3, 1, 2, 4)
    xc = xc.reshape(batch, 2 * NSA_GROUPS, N_CMP_PAD, CMP_STRIDE * NSA_DH)
    pe = p["cmp_pe"]
    pea = pe[:CMP_STRIDE].reshape(1, -1)
    peb = pe[CMP_STRIDE:].reshape(1, -1)
    w1 = jnp.stack([p["cmp_w1_k"], p["cmp_w1_v"]])
    w1a = w1[:, :CMP_STRIDE].reshape(2, CMP_STRIDE * NSA_DH, NSA_DH).astype(BF16)
    w1b = w1[:, CMP_STRIDE:].reshape(2, CMP_STRIDE * NSA_DH, NSA_DH).astype(BF16)
    w2 = jnp.stack([p["cmp_w2_k"], p["cmp_w2_v"]])
    w2p = jnp.concatenate([w2, jnp.zeros_like(w2)], axis=2).astype(BF16)
    kcvc = _compress(xc, pea, peb, w1a, w1b, w2p, tabs["cmp"])

    q3 = q.reshape(batch, SEQ, BRANCH)
    ocmp, selbias = _cmp_attn(q3, kcvc, tabs["ovt"])
    osel, owin = _sel_win(q3, kv2.reshape(batch, SEQ, KV_LANES), selbias)

    yc = _retention(rq.reshape(batch, SEQ, 3 * BRANCH), rg.reshape(batch, SEQ, BRANCH), *tabs["ret"],
                    p["ret_gn_g"][None, :])

    x2 = _merge(x2, p["norm1_g"][None, :], w_gate, p["b_gate"][None, :], ya, ocmp.reshape(n, BRANCH),
                osel.reshape(n, BRANCH), owin.reshape(n, BRANCH), ng, yc.reshape(n, BRANCH), yd,
                p["w_branch"].astype(BF16), p["w_o"].astype(BF16))
    return _mlp(x2, p["norm2_g"][None, :], p["w_up"].astype(BF16), p["ffn_conv"], p["w_down"].astype(BF16),
                final_g[None, :], final_norm)


def kernel(x, norm1_g, w_in, b_gate, sc_conv, cmp_pe, cmp_w1_k, cmp_w2_k, cmp_w1_v, cmp_w2_v, ret_gn_g, pool_w, pool_scale, w_branch, w_o, norm2_g, w_up, ffn_conv, w_down, final_norm_g):
    batch, seq, d = x.shape
    assert seq == SEQ and d == D_MODEL
    depth = w_in.shape[0]
    params = dict(norm1_g=norm1_g, w_in=w_in, b_gate=b_gate, sc_conv=sc_conv, cmp_pe=cmp_pe, cmp_w1_k=cmp_w1_k,
                  cmp_w2_k=cmp_w2_k, cmp_w1_v=cmp_w1_v, cmp_w2_v=cmp_w2_v, ret_gn_g=ret_gn_g, pool_w=pool_w,
                  pool_scale=pool_scale, w_branch=w_branch, w_o=w_o, norm2_g=norm2_g, w_up=w_up, ffn_conv=ffn_conv,
                  w_down=w_down)
    pos = jnp.arange(SEQ)
    cmp_end = jnp.asarray(np.arange(N_CMP_PAD) * CMP_STRIDE + CMP_BLOCK - 1)
    tabs = dict(tok=_nsa_rope_tables(pos) + _ret_rope_tables(pos), cmp=_nsa_rope_tables(cmp_end),
                ret=_retention_decay_tables(), ovt=_overlap_t(), onehot=_block_onehot())
    x2 = x.reshape(batch * seq, d)
    for l in range(depth):
        p = {k: v[l] for k, v in params.items()}
        x2 = _layer(x2, batch, p, tabs, final_norm_g, final_norm=(l == depth - 1))
    return x2.reshape(batch, seq, d)
```

```python
import functools

import numpy as np
import jax
import jax.numpy as jnp
from jax import lax
from jax.experimental import pallas as pl
from jax.experimental.pallas import tpu as pltpu

F32 = jnp.float32
BF16 = jnp.bfloat16

D_MODEL = 1024
SEQ = 2048
BRANCH = 512
N_BRANCH = 4
CONV_TAPS = 3
NSA_HEADS = 8
NSA_GROUPS = 2
NSA_HPG = NSA_HEADS // NSA_GROUPS
NSA_DH = BRANCH // NSA_HEADS
CMP_BLOCK = 32
CMP_STRIDE = 16
N_CMP = (SEQ - CMP_BLOCK) // CMP_STRIDE + 1
N_CMP_PAD = SEQ // CMP_STRIDE
SEL_BLOCK = 64
N_SEL = SEQ // SEL_BLOCK
SEL_TOP_N = 16
WINDOW = 512
ROPE_THETA = 500000.0
ROPE_DIM = NSA_DH // 4
FORCE_SCORE = 1.0e4
RET_HEADS = 4
RET_DK = BRANCH // RET_HEADS
RET_CHUNK = 128
RET_THETA = 10000.0
POOL_WINDOWS = (2, 4, 8, 16)
POOL_GROUP_DIM = BRANCH // len(POOL_WINDOWS)
D_FF = 2816
EPS = 1e-6
NEG_INF = -1e30
LOG2_E = 1.4426950408889634
Q_SCALE = NSA_DH ** -0.5 * LOG2_E

SC_OFF = 0
Q_OFF = 3 * BRANCH
KV_OFF = Q_OFF + BRANCH
NG_OFF = KV_OFF + 3 * 2 * NSA_GROUPS * NSA_DH
NG_COLS = 3 * NSA_HEADS
NG_PAD = 128
RET_OFF = NG_OFF + NG_PAD
POOL_OFF = RET_OFF + 4 * BRANCH
MAIN_COLS = POOL_OFF + BRANCH
GATE_OFF_ORIG = NG_OFF + NG_COLS + 4 * BRANCH + BRANCH

TOK_TILE = 512
TILES_PER_SEQ = SEQ // TOK_TILE
CONV_HALO = 8
POOL_HALO = 16
ATT_TQ = 256
ATT_TK = 256
KV_LANES = 4 * NSA_GROUPS * 128
CMP_TQ = 512
VMEM_LIMIT = 56 * 1024 * 1024


def _const_spec(shape):
    n = len(shape)
    return pl.BlockSpec(shape, lambda *_: (0,) * n, pipeline_mode=pl.Buffered(1))


def _layer_spec(layer, shape):
    n = len(shape)
    return pl.BlockSpec((None,) + tuple(shape), lambda *_: (layer,) + (0,) * n, pipeline_mode=pl.Buffered(1))


def _rms(x, g):
    ms = jnp.mean(x * x, axis=-1, keepdims=True)
    return x * lax.rsqrt(ms + EPS) * g


def _dot(a, b):
    return jnp.dot(a, b, preferred_element_type=F32)


def _dot_nt(a, b):
    return lax.dot_general(a, b, (((1,), (1,)), ((), ())), preferred_element_type=F32)


def _inproj_kernel(x_ref, g_ref, w_ref, scw_ref, nc_ref, nsa_ref, nsb_ref, rc_ref, rs_ref, oh_ref, pw_ref, ps_ref,
                   ya_ref, q_ref, kc_ref, vc_ref, kv2_ref, ng_ref, rq_ref, rg_ref, yd_ref, ch_buf, u_buf):
    T = TOK_TILE
    i = pl.program_id(0)

    @pl.when(i % TILES_PER_SEQ == 0)
    def _():
        ch_buf[0:CONV_HALO, :] = jnp.zeros((CONV_HALO, BRANCH), F32)
        u_buf[0:POOL_HALO, :] = jnp.zeros((POOL_HALO, BRANCH), F32)

    h = _rms(x_ref[...], g_ref[...]).astype(BF16)

    def proj(off, width):
        return _dot(h, w_ref[:, off:off + width])

    zb = proj(SC_OFF, BRANCH)
    ch_buf[CONV_HALO:CONV_HALO + T, :] = proj(SC_OFF + BRANCH, BRANCH) * proj(SC_OFF + 2 * BRANCH, BRANCH)
    conv = scw_ref[CONV_TAPS - 1:CONV_TAPS, :] * ch_buf[CONV_HALO:CONV_HALO + T, :]
    for k in range(1, CONV_TAPS):
        conv = conv + scw_ref[CONV_TAPS - 1 - k:CONV_TAPS - k, :] * ch_buf[CONV_HALO - k:CONV_HALO - k + T, :]
    ya_ref[...] = (zb * conv).astype(BF16)
    ch_buf[0:CONV_HALO, :] = ch_buf[T:T + CONV_HALO, :]

    nc, nsa, nsb = nc_ref[...], nsa_ref[...], nsb_ref[...]

    def rope_nsa(x):
        return x * nc + pltpu.roll(x, 128 - ROPE_DIM // 2, 1) * nsa + pltpu.roll(x, ROPE_DIM // 2, 1) * nsb

    zq = proj(Q_OFF, BRANCH)
    for c in range(BRANCH // 128):
        q_ref[:, c * 128:(c + 1) * 128] = (rope_nsa(zq[:, c * 128:(c + 1) * 128]) * Q_SCALE).astype(BF16)
    zkv = proj(KV_OFF, 768)
    kc_ref[...] = zkv[:, 0:128]
    vc_ref[...] = zkv[:, 128:256]
    pieces = (rope_nsa(zkv[:, 256:384]).astype(BF16), zkv[:, 384:512].astype(BF16),
              rope_nsa(zkv[:, 512:640]).astype(BF16), zkv[:, 640:768].astype(BF16))
    fills = (oh_ref[...], jnp.ones((T, NSA_DH), BF16), jnp.zeros((T, NSA_DH), BF16), jnp.ones((T, NSA_DH), BF16))
    for c, (piece, fill) in enumerate(zip(pieces, fills)):
        for g in range(NSA_GROUPS):
            kv2_ref[:, (2 * c + g) * 128:(2 * c + g + 1) * 128] = jnp.concatenate(
                [piece[:, g * NSA_DH:(g + 1) * NSA_DH], fill], axis=1)
    ng_ref[...] = jax.nn.sigmoid(proj(NG_OFF, NG_PAD))

    rc, rs = rc_ref[...], rs_ref[...]

    def rope_ret(x):
        return x * rc + pltpu.roll(x, RET_DK // 2, 1) * rs

    zr = proj(RET_OFF, BRANCH)
    for c in range(RET_HEADS):
        rq_ref[:, c * 128:(c + 1) * 128] = rope_ret(zr[:, c * 128:(c + 1) * 128]).astype(BF16)
    zr = proj(RET_OFF + BRANCH, BRANCH)
    for c in range(RET_HEADS):
        rq_ref[:, BRANCH + c * 128:BRANCH + (c + 1) * 128] = (
            rope_ret(zr[:, c * 128:(c + 1) * 128]) * (RET_DK ** -0.5)).astype(BF16)
    rq_ref[:, 2 * BRANCH:3 * BRANCH] = proj(RET_OFF + 2 * BRANCH, BRANCH).astype(BF16)
    rg_ref[...] = proj(RET_OFF + 3 * BRANCH, BRANCH)

    zu = proj(POOL_OFF, BRANCH)
    u_buf[POOL_HALO:POOL_HALO + T, :] = zu
    pos = (i % TILES_PER_SEQ) * T + lax.broadcasted_iota(jnp.int32, (T, 1), 0)
    for gi, win in enumerate(POOL_WINDOWS):
        c0, c1 = gi * POOL_GROUP_DIM, (gi + 1) * POOL_GROUP_DIM
        s = u_buf[:, c0:c1]
        shift = 1
        while shift < win:
            s = s + pltpu.roll(s, shift, 0)
            shift *= 2
        s = s[POOL_HALO:, :]
        cnt = jnp.minimum(pos + 1, win).astype(F32)
        pooled = s / cnt - zu[:, c0:c1]
        y = _dot(pooled.astype(BF16), pw_ref[gi]) * ps_ref[:, c0:c1]
        yd_ref[:, c0:c1] = y.astype(BF16)
    u_buf[0:POOL_HALO, :] = u_buf[T:T + POOL_HALO, :]


def _inproj(x2, g, w_main, layer, scw, tabs, onehot, pool_w, pool_scale):
    n = x2.shape[0]
    T = TOK_TILE
    tok = lambda width: pl.BlockSpec((T, width), lambda i: (i, 0))
    tab = pl.BlockSpec((T, 128), lambda i: (i % TILES_PER_SEQ, 0))
    out_shapes = (
        jax.ShapeDtypeStruct((n, BRANCH), BF16),
        jax.ShapeDtypeStruct((n, BRANCH), BF16),
        jax.ShapeDtypeStruct((n, 128), F32),
        jax.ShapeDtypeStruct((n, 128), F32),
        jax.ShapeDtypeStruct((n, KV_LANES), BF16),
        jax.ShapeDtypeStruct((n, NG_PAD), F32),
        jax.ShapeDtypeStruct((n, 3 * BRANCH), BF16),
        jax.ShapeDtypeStruct((n, BRANCH), F32),
        jax.ShapeDtypeStruct((n, BRANCH), BF16),
    )
    return pl.pallas_call(
        _inproj_kernel,
        name="inproj",
        grid=(n // T,),
        in_specs=[tok(D_MODEL), _const_spec((1, D_MODEL)), _layer_spec(layer, (D_MODEL, MAIN_COLS)),
                  _const_spec((CONV_TAPS, BRANCH)), tab, tab, tab, tab, tab,
                  pl.BlockSpec((T, NSA_DH), lambda i: (i % TILES_PER_SEQ, 0)),
                  _layer_spec(layer, (len(POOL_WINDOWS), POOL_GROUP_DIM, POOL_GROUP_DIM)), _const_spec((1, BRANCH))],
        out_specs=[tok(BRANCH), tok(BRANCH), tok(128), tok(128), tok(KV_LANES), tok(NG_PAD), tok(3 * BRANCH), tok(BRANCH),
                   tok(BRANCH)],
        out_shape=out_shapes,
        scratch_shapes=[pltpu.VMEM((CONV_HALO + T, BRANCH), F32), pltpu.VMEM((POOL_HALO + T, BRANCH), F32)],
        compiler_params=pltpu.CompilerParams(dimension_semantics=("arbitrary",), vmem_limit_bytes=VMEM_LIMIT),
    )(x2, g, w_main, scw, *tabs, onehot, pool_w, pool_scale)


def _compress_kernel(kc_ref, vc_ref, pe_ref, w1_ref, w1f_ref, w2_ref, cc_ref, csa_ref, csb_ref, out_ref):
    for kv, x_ref in enumerate((kc_ref, vc_ref)):
        acc = jnp.zeros((N_CMP_PAD, 4 * NSA_DH), F32)
        for l in range(CMP_STRIDE):
            x = x_ref[0, pl.ds(l, N_CMP_PAD, stride=CMP_STRIDE), :].astype(BF16)
            acc = acc + _dot(x, w1_ref[kv, l])
        bias = _dot(pe_ref[...], w1f_ref[kv])[0:1, :]
        for g in range(NSA_GROUPS):
            a = acc[:, g * 128:(g + 1) * 128]
            hid = jax.nn.gelu(a + pltpu.roll(pltpu.roll(a, N_CMP_PAD - 1, 0), NSA_DH, 1) + bias)
            y = _dot(hid.astype(BF16), w2_ref[kv])
            if kv == 0:
                y = (y * cc_ref[...] + pltpu.roll(y, 128 - ROPE_DIM // 2, 1) * csa_ref[...]
                     + pltpu.roll(y, ROPE_DIM // 2, 1) * csb_ref[...])
            out_ref[0, kv * NSA_GROUPS + g] = y[:, 0:NSA_DH].astype(BF16)


def _compress(kc3, vc3, pe8, w1blk, w1f, w2p, ctabs):
    b = kc3.shape[0]
    return pl.pallas_call(
        _compress_kernel,
        name="compress",
        grid=(b,),
        in_specs=[pl.BlockSpec((1, SEQ, 128), lambda i: (i, 0, 0)), pl.BlockSpec((1, SEQ, 128), lambda i: (i, 0, 0)),
                  _const_spec((8, CMP_BLOCK * NSA_DH)), _const_spec((2, CMP_STRIDE, 128, 4 * NSA_DH)),
                  _const_spec((2, CMP_BLOCK * NSA_DH, 128)), _const_spec((2, 128, 128)),
                  _const_spec((N_CMP_PAD, 128)), _const_spec((N_CMP_PAD, 128)), _const_spec((N_CMP_PAD, 128))],
        out_specs=pl.BlockSpec((1, 4, N_CMP_PAD, NSA_DH), lambda i: (i, 0, 0, 0)),
        out_shape=jax.ShapeDtypeStruct((b, 4, N_CMP_PAD, NSA_DH), BF16),
        compiler_params=pltpu.CompilerParams(dimension_semantics=("arbitrary",), vmem_limit_bytes=VMEM_LIMIT),
    )(kc3, vc3, pe8, w1blk, w1f, w2p, *ctabs)


def _cmp_attn_kernel(q_ref, kc_ref, vc_ref, ovt_ref, o_ref, sb_ref):
    R = CMP_TQ
    r = pl.program_id(2)
    kc, vc = kc_ref[0, 0], vc_ref[0, 0]
    t_rows = r * R + lax.broadcasted_iota(jnp.int32, (R, N_CMP_PAD), 0)
    blk_end = lax.broadcasted_iota(jnp.int32, (R, N_CMP_PAD), 1) * CMP_STRIDE + (CMP_BLOCK - 1)
    valid = blk_end <= t_rows
    qs = jnp.concatenate([q_ref[0, :, j * NSA_DH:(j + 1) * NSA_DH] for j in range(NSA_HPG)], axis=0)
    sm = jnp.where(valid[None], _dot_nt(qs, kc).reshape(NSA_HPG, R, N_CMP_PAD), NEG_INF)
    e = jnp.exp2(sm - jnp.max(sm, axis=-1, keepdims=True))
    p = jnp.where(valid[None], e, 0.0) / jnp.sum(e, axis=-1, keepdims=True)
    o = _dot(p.reshape(NSA_HPG * R, N_CMP_PAD).astype(BF16), vc)
    for j in range(NSA_HPG):
        o_ref[0, :, j * NSA_DH:(j + 1) * NSA_DH] = o[j * R:(j + 1) * R]
    psum = jnp.sum(p, axis=0)
    imp = lax.dot_general(ovt_ref[...], psum, (((1,), (1,)), ((), ())), preferred_element_type=F32,
                          precision=lax.Precision.HIGHEST)
    cur = (r * R + lax.broadcasted_iota(jnp.int32, (N_SEL, R), 1)) // SEL_BLOCK
    bid = lax.broadcasted_iota(jnp.int32, (N_SEL, R), 0)
    forced = (bid == 0) | (bid == cur) | (bid == cur - 1)
    imp = jnp.where(forced, FORCE_SCORE, jnp.where(bid > cur, -FORCE_SCORE, imp))
    rank = jnp.zeros((N_SEL, R), F32)
    for m in range(N_SEL):
        a = imp[m:m + 1, :]
        before = jnp.where(bid > m, jnp.where(a >= imp, 1.0, 0.0), jnp.where(a > imp, 1.0, 0.0))
        rank = rank + before
    bias_t = jnp.where(rank < SEL_TOP_N, 0.0, NEG_INF)
    bias = bias_t.T
    sb_ref[0, 0] = jnp.concatenate([bias, jnp.zeros((R, NSA_DH - N_SEL), F32)], axis=1).astype(BF16)


def _cmp_attn(q3, kcvc, ovt):
    b = q3.shape[0]
    R = CMP_TQ
    gw = NSA_HPG * NSA_DH
    return pl.pallas_call(
        _cmp_attn_kernel,
        name="cmp_attn",
        grid=(b, NSA_GROUPS, SEQ // R),
        in_specs=[pl.BlockSpec((1, R, gw), lambda i, g, r: (i, r, g)),
                  pl.BlockSpec((1, 1, N_CMP_PAD, NSA_DH), lambda i, g, r: (i, g, 0, 0)),
                  pl.BlockSpec((1, 1, N_CMP_PAD, NSA_DH), lambda i, g, r: (i, NSA_GROUPS + g, 0, 0)),
                  _const_spec((N_SEL, N_CMP_PAD))],
        out_specs=[pl.BlockSpec((1, R, gw), lambda i, g, r: (i, r, g)),
                   pl.BlockSpec((1, 1, R, NSA_DH), lambda i, g, r: (i, g, r, 0))],
        out_shape=(jax.ShapeDtypeStruct((b, SEQ, BRANCH), F32),
                   jax.ShapeDtypeStruct((b, NSA_GROUPS, SEQ, NSA_DH), BF16)),
        compiler_params=pltpu.CompilerParams(dimension_semantics=("arbitrary",) * 3, vmem_limit_bytes=VMEM_LIMIT),
    )(q3, kcvc, kcvc, ovt)


def _sel_win_kernel(q_ref, kv_ref, sb_ref, osel_ref, owin_ref, qa_ref, m_ref, acc_ref):
    TQ, TK = ATT_TQ, ATT_TK
    M = NSA_HPG * TQ
    qi = pl.program_id(1)

    def chunk(rows, c, g):
        return kv_ref[0, rows, (2 * c + g) * 128:(2 * c + g + 1) * 128]

    for g in range(NSA_GROUPS):
        sb = sb_ref[0, g]
        for j in range(NSA_HPG):
            h = g * NSA_HPG + j
            qa_ref[g, j * TQ:(j + 1) * TQ, :] = jnp.concatenate([q_ref[0, :, h * NSA_DH:(h + 1) * NSA_DH], sb], axis=1)

    m_ref[...] = jnp.full(m_ref.shape, NEG_INF, F32)
    acc_ref[...] = jnp.zeros(acc_ref.shape, F32)

    def online(idx, s, v):
        m = m_ref[idx]
        m_new = jnp.maximum(m, jnp.max(s, axis=-1, keepdims=True))
        p = jnp.exp2(s - pltpu.repeat(m_new, s.shape[1] // 128, axis=1))
        acc_ref[idx] = jnp.exp2(m - m_new) * acc_ref[idx] + _dot(p.astype(BF16), v)
        m_ref[idx] = m_new

    def far_tiles(k0, width):
        rows = pl.ds(pl.multiple_of(k0, TK), width)
        for g in range(NSA_GROUPS):
            online(g, _dot_nt(qa_ref[g], chunk(rows, 0, g)), chunk(rows, 1, g))

    def near_tile(t, carry):
        rows = pl.ds(pl.multiple_of(t * TK, TK), TK)
        dist = (lax.broadcasted_iota(jnp.int32, (TQ, TK), 0) - lax.broadcasted_iota(jnp.int32, (TQ, TK), 1)
                + (qi - t) * TK)
        causal_bias = jnp.where(dist >= 0, 0.0, NEG_INF)
        window_bias = jnp.where(lax.bitcast_convert_type(dist, jnp.uint32) < WINDOW, 0.0, NEG_INF)

        def masked(s, bias):
            return (s.reshape(NSA_HPG, TQ, TK) + bias[None]).reshape(M, TK)

        for g in range(NSA_GROUPS):
            qa = qa_ref[g]
            online(g, masked(_dot_nt(qa, chunk(rows, 0, g)), causal_bias), chunk(rows, 1, g))
            online(NSA_GROUPS + g, masked(_dot_nt(qa, chunk(rows, 2, g)), window_bias), chunk(rows, 3, g))
        return carry

    n_far = jnp.maximum(qi - WINDOW // TK, 0)

    def far_pair(i, carry):
        far_tiles(i * (2 * TK), 2 * TK)
        return carry

    lax.fori_loop(0, n_far // 2, far_pair, 0)

    @pl.when(n_far % 2 == 1)
    def _():
        far_tiles((n_far - 1) * TK, TK)

    lax.fori_loop(n_far, qi + 1, near_tile, 0)

    lane = lax.broadcasted_iota(jnp.int32, (TQ, 128), 1)
    for idx, o_ref in ((0, osel_ref), (NSA_GROUPS, owin_ref)):
        for g in range(NSA_GROUPS):
            halves = []
            for j in range(NSA_HPG):
                acc = acc_ref[idx + g, j * TQ:(j + 1) * TQ, :]
                rolled = pltpu.roll(acc, NSA_DH, 1)
                halves.append(acc * (1.0 / rolled) if j % 2 == 0 else rolled * (1.0 / acc))
            for c in range(NSA_HPG // 2):
                o_ref[0, :, (g * 2 + c) * 128:(g * 2 + c + 1) * 128] = jnp.where(
                    lane < NSA_DH, halves[2 * c], halves[2 * c + 1])


def _sel_win(q3, kv3, selbias):
    b = q3.shape[0]
    TQ = ATT_TQ
    ospec = pl.BlockSpec((1, TQ, BRANCH), lambda i, t: (i, t, 0))
    return pl.pallas_call(
        _sel_win_kernel,
        name="sel_win_attn",
        grid=(b, SEQ // TQ),
        in_specs=[pl.BlockSpec((1, TQ, BRANCH), lambda i, t: (i, t, 0)),
                  pl.BlockSpec((1, SEQ, KV_LANES), lambda i, t: (i, 0, 0)),
                  pl.BlockSpec((1, NSA_GROUPS, TQ, NSA_DH), lambda i, t: (i, 0, t, 0))],
        out_specs=[ospec, ospec],
        out_shape=(jax.ShapeDtypeStruct((b, SEQ, BRANCH), F32), jax.ShapeDtypeStruct((b, SEQ, BRANCH), F32)),
        scratch_shapes=[pltpu.VMEM((NSA_GROUPS, NSA_HPG * TQ, 128), BF16),
                        pltpu.VMEM((2 * NSA_GROUPS, NSA_HPG * TQ, 128), F32),
                        pltpu.VMEM((2 * NSA_GROUPS, NSA_HPG * TQ, 128), F32)],
        compiler_params=pltpu.CompilerParams(dimension_semantics=("arbitrary", "arbitrary"),
                                             vmem_limit_bytes=VMEM_LIMIT),
    )(q3, kv3, selbias)


def _retention_kernel(q_ref, k_ref, v_ref, zg_ref, intra_ref, qd_ref, kd_ref, cd_ref, gn_ref, y_ref, st_ref):
    C = RET_CHUNK
    NC = SEQ // C

    def bdot(a, b, ca, cb):
        return lax.dot_general(a, b, (((ca,), (cb,)), ((0,), (0,))), preferred_element_type=F32)

    q3, k3, v3 = (r[0].reshape(NC, C, RET_DK) for r in (q_ref, k_ref, v_ref))
    scores = bdot(q3, k3, 2, 2) * intra_ref[0][None]
    inner = bdot(scores.astype(BF16), v3, 2, 1)
    kdec = (k3.astype(F32) * kd_ref[0][None]).astype(BF16)
    kv = bdot(kdec, v3, 1, 1)
    cd = cd_ref[0, 0:1, :]
    state = jnp.zeros((RET_DK, RET_DK), F32)
    for c in range(NC):
        st_ref[c] = state.astype(BF16)
        state = state * cd + kv[c]
    qdec = (q3.astype(F32) * qd_ref[0][None]).astype(BF16)
    o = inner + bdot(qdec, st_ref[...], 2, 1)
    mu = jnp.mean(o, axis=-1, keepdims=True)
    d = o - mu
    var = jnp.mean(d * d, axis=-1, keepdims=True)
    o = (d * lax.rsqrt(var + EPS)).reshape(SEQ, RET_DK) * gn_ref[...]
    y_ref[0] = (o * jax.nn.silu(zg_ref[0])).astype(BF16)


def _retention(rq3, rg3, intra, qd, kd, cd, gn):
    b = rq3.shape[0]
    hspec = lambda off: pl.BlockSpec((1, SEQ, RET_DK), lambda i, h: (i, 0, off + h))
    tspec = pl.BlockSpec((1, RET_CHUNK, RET_DK), lambda i, h: (h, 0, 0))
    return pl.pallas_call(
        _retention_kernel,
        name="retention",
        grid=(b, RET_HEADS),
        in_specs=[hspec(0), hspec(RET_HEADS), hspec(2 * RET_HEADS), hspec(0), tspec, tspec, tspec,
                  pl.BlockSpec((1, 8, RET_DK), lambda i, h: (h, 0, 0)),
                  pl.BlockSpec((1, RET_DK), lambda i, h: (0, h))],
        out_specs=hspec(0),
        out_shape=jax.ShapeDtypeStruct((b, SEQ, BRANCH), BF16),
        scratch_shapes=[pltpu.VMEM((SEQ // RET_CHUNK, RET_DK, RET_DK), BF16)],
        compiler_params=pltpu.CompilerParams(dimension_semantics=("arbitrary", "arbitrary"),
                                             vmem_limit_bytes=VMEM_LIMIT),
    )(rq3, rq3, rq3, rg3, intra, qd, kd, cd, gn)


def _merge_kernel(x_ref, g_ref, wg_ref, bg_ref, ya_ref, oc_ref, os_ref, ow_ref, ng_ref, yc_ref, yd_ref, wb_ref, wo_ref,
                  o_ref):
    T = TOK_TILE
    x = x_ref[...]
    h = _rms(x, g_ref[...]).astype(BF16)
    ng = ng_ref[...]
    yb = None
    for br, o_br in enumerate((oc_ref, os_ref, ow_ref)):
        gate = jnp.concatenate([jnp.broadcast_to(ng[:, 3 * hd + br:3 * hd + br + 1], (T, NSA_DH))
                                for hd in range(NSA_HEADS)], axis=1)
        yb = gate * o_br[...] if yb is None else yb + gate * o_br[...]
    ys = (ya_ref[...], yb.astype(BF16), yc_ref[...], yd_ref[...])
    merged = None
    for n, y in enumerate(ys):
        gate = jax.nn.sigmoid(_dot(h, wg_ref[:, n * D_MODEL:(n + 1) * D_MODEL])
                              + bg_ref[:, n * D_MODEL:(n + 1) * D_MODEL])
        term = _dot(y, wb_ref[n]) * gate
        merged = term if merged is None else merged + term
    o_ref[...] = x + _dot(merged.astype(BF16), wo_ref[...])


def _merge(x2, g, wg, layer, bg, ya, oc, osel, owin, ng, yc, yd, wb, wo):
    n = x2.shape[0]
    T = TOK_TILE
    tok = lambda width: pl.BlockSpec((T, width), lambda i: (i, 0))
    return pl.pallas_call(
        _merge_kernel,
        name="merge",
        grid=(n // T,),
        in_specs=[tok(D_MODEL), _const_spec((1, D_MODEL)), _layer_spec(layer, (D_MODEL, N_BRANCH * D_MODEL)),
                  _const_spec((1, N_BRANCH * D_MODEL)), tok(BRANCH), tok(BRANCH), tok(BRANCH), tok(BRANCH), tok(NG_PAD),
                  tok(BRANCH), tok(BRANCH),
                  _layer_spec(layer, (N_BRANCH, BRANCH, D_MODEL)), _layer_spec(layer, (D_MODEL, D_MODEL))],
        out_specs=tok(D_MODEL),
        out_shape=jax.ShapeDtypeStruct((n, D_MODEL), F32),
        compiler_params=pltpu.CompilerParams(dimension_semantics=("arbitrary",), vmem_limit_bytes=VMEM_LIMIT),
    )(x2, g, wg, bg, ya, oc, osel, owin, ng, yc, yd, wb, wo)


FF_CHUNK = 256


def _mlp_kernel(x_ref, g_ref, wup_ref, cw_ref, wdn_ref, fg_ref, o_ref, ug_buf, act_buf, *, final_norm):
    T = TOK_TILE
    i = pl.program_id(0)

    @pl.when(i % TILES_PER_SEQ == 0)
    def _():
        ug_buf[0:CONV_HALO, :] = jnp.zeros((CONV_HALO, D_FF), F32)

    x = x_ref[...]
    h = _rms(x, g_ref[...]).astype(BF16)
    for c in range(D_FF // FF_CHUNK):
        cols = slice(c * FF_CHUNK, (c + 1) * FF_CHUNK)
        ug_buf[CONV_HALO:CONV_HALO + T, cols] = _dot(h, wup_ref[:, cols])
        conv = cw_ref[CONV_TAPS - 1:CONV_TAPS, cols] * ug_buf[CONV_HALO:CONV_HALO + T, cols]
        for k in range(1, CONV_TAPS):
            conv = conv + cw_ref[CONV_TAPS - 1 - k:CONV_TAPS - k, cols] * ug_buf[CONV_HALO - k:CONV_HALO - k + T, cols]
        val = _dot(h, wup_ref[:, D_FF + c * FF_CHUNK:D_FF + (c + 1) * FF_CHUNK])
        act_buf[:, cols] = (jax.nn.silu(conv) * val).astype(BF16)
    ug_buf[0:CONV_HALO, :] = ug_buf[T:T + CONV_HALO, :]
    y = x + _dot(act_buf[...], wdn_ref[...])
    if final_norm:
        y = _rms(y, fg_ref[...])
    o_ref[...] = y


def _mlp(x2, g, wup, layer, cw, wdn, fg, final_norm):
    n = x2.shape[0]
    T = TOK_TILE
    tok = pl.BlockSpec((T, D_MODEL), lambda i: (i, 0))
    return pl.pallas_call(
        functools.partial(_mlp_kernel, final_norm=final_norm),
        name="mlp",
        grid=(n // T,),
        in_specs=[tok, _const_spec((1, D_MODEL)), _layer_spec(layer, (D_MODEL, 2 * D_FF)), _const_spec((CONV_TAPS, D_FF)),
                  _layer_spec(layer, (D_FF, D_MODEL)), _const_spec((1, D_MODEL))],
        out_specs=tok,
        out_shape=jax.ShapeDtypeStruct((n, D_MODEL), F32),
        scratch_shapes=[pltpu.VMEM((CONV_HALO + T, D_FF), F32), pltpu.VMEM((T, D_FF), BF16)],
        compiler_params=pltpu.CompilerParams(dimension_semantics=("arbitrary",), vmem_limit_bytes=VMEM_LIMIT),
    )(x2, g, wup, cw, wdn, fg)


def _rope_angles(pos, half, theta):
    inv_freq = np.power(np.float32(theta), -np.arange(half, dtype=np.float32) / np.float32(half))
    ang = pos.astype(F32)[:, None] * jnp.asarray(inv_freq)[None, :]
    return jnp.cos(ang), jnp.sin(ang)


def _nsa_rope_tables(pos):
    half = ROPE_DIM // 2
    cos, sin = _rope_angles(pos, half, ROPE_THETA)
    n = pos.shape[0]
    ones = jnp.ones((n, NSA_DH - ROPE_DIM), F32)
    zeros = jnp.zeros((n, NSA_DH - ROPE_DIM), F32)
    zh = jnp.zeros((n, half), F32)
    c = jnp.concatenate([cos, cos, ones], axis=1)
    sa = jnp.concatenate([-sin, zh, zeros], axis=1)
    sb = jnp.concatenate([zh, sin, zeros], axis=1)
    return tuple(jnp.tile(t, (1, 128 // NSA_DH)) for t in (c, sa, sb))


def _ret_rope_tables(pos):
    half = RET_DK // 2
    cos, sin = _rope_angles(pos, half, RET_THETA)
    return jnp.concatenate([cos, cos], axis=1), jnp.concatenate([-sin, sin], axis=1)


def _retention_decay_tables():
    H, C = RET_HEADS, RET_CHUNK
    log_g = np.log1p(-np.exp2(-5.0 - np.arange(H))).astype(np.float32)
    n = np.arange(C, dtype=np.float32)
    diff = n[:, None] - n[None, :]
    intra = np.where(diff >= 0, np.exp(np.maximum(diff, 0.0)[None] * log_g[:, None, None]), 0.0).astype(np.float32)
    q_dec = np.exp((n[None, :] + 1.0) * log_g[:, None]).astype(np.float32)
    k_dec = np.exp((C - 1.0 - n[None, :]) * log_g[:, None]).astype(np.float32)
    c_dec = np.exp(C * log_g).astype(np.float32)
    qd = np.broadcast_to(q_dec[:, :, None], (H, C, RET_DK))
    kd = np.broadcast_to(k_dec[:, :, None], (H, C, RET_DK))
    cd = np.broadcast_to(c_dec[:, None, None], (H, 8, RET_DK))
    return tuple(jnp.asarray(np.ascontiguousarray(t)) for t in (intra, qd, kd, cd))


def _overlap_t():
    cs0 = np.arange(N_CMP_PAD) * CMP_STRIDE
    ss0 = np.arange(N_SEL) * SEL_BLOCK
    ov = np.clip(np.minimum(cs0[:, None] + CMP_BLOCK, ss0[None, :] + SEL_BLOCK)
                 - np.maximum(cs0[:, None], ss0[None, :]), 0, None)
    ov = (ov / CMP_BLOCK).astype(np.float32)
    ov[N_CMP:, :] = 0.0
    return jnp.asarray(np.ascontiguousarray(ov.T))


def _block_onehot():
    e = np.zeros((SEQ, NSA_DH), np.float32)
    e[np.arange(SEQ), np.arange(SEQ) // SEL_BLOCK] = 1.0
    return jnp.asarray(e, dtype=BF16)


def _layer(x2, batch, layer, p, wts, tabs, final_g, final_norm):
    n = x2.shape[0]
    ya, q, kc_in, vc_in, kv2, ng, rq, rg, yd = _inproj(
        x2, p["norm1_g"][None, :], wts["main"], layer, p["sc_conv"], tabs["tok"], tabs["onehot"], wts["pool"],
        p["pool_scale"][None, :])

    w1 = jnp.stack([p["cmp_w1_k"], p["cmp_w1_v"]])
    wcat = jnp.concatenate([w1[:, :CMP_STRIDE], w1[:, CMP_STRIDE:]], axis=3)
    zero = jnp.zeros_like(wcat)
    w1blk = jnp.concatenate([jnp.concatenate([wcat, zero], axis=3), jnp.concatenate([zero, wcat], axis=3)],
                            axis=2).astype(BF16)
    w1f = w1.reshape(2, CMP_BLOCK * NSA_DH, NSA_DH)
    w1f = jnp.concatenate([w1f, jnp.zeros_like(w1f)], axis=2).astype(BF16)
    pe8 = jnp.broadcast_to(p["cmp_pe"].reshape(1, -1), (8, CMP_BLOCK * NSA_DH)).astype(BF16)
    w2 = jnp.stack([p["cmp_w2_k"], p["cmp_w2_v"]])
    w2p = jnp.pad(w2, ((0, 0), (0, 128 - NSA_DH), (0, 128 - NSA_DH))).astype(BF16)
    kcvc = _compress(kc_in.reshape(batch, SEQ, 128), vc_in.reshape(batch, SEQ, 128), pe8, w1blk, w1f, w2p,
                     tabs["cmp"])

    q3 = q.reshape(batch, SEQ, BRANCH)
    ocmp, selbias = _cmp_attn(q3, kcvc, tabs["ovt"])
    osel, owin = _sel_win(q3, kv2.reshape(batch, SEQ, KV_LANES), selbias)

    yc = _retention(rq.reshape(batch, SEQ, 3 * BRANCH), rg.reshape(batch, SEQ, BRANCH), *tabs["ret"],
                    p["ret_gn_g"][None, :])

    x2 = _merge(x2, p["norm1_g"][None, :], wts["gate"], layer, p["b_gate"][None, :], ya, ocmp.reshape(n, BRANCH),
                osel.reshape(n, BRANCH), owin.reshape(n, BRANCH), ng, yc.reshape(n, BRANCH), yd,
                wts["branch"], wts["o"])
    return _mlp(x2, p["norm2_g"][None, :], wts["up"], layer, p["ffn_conv"], wts["down"], final_g[None, :], final_norm)


def kernel(x, norm1_g, w_in, b_gate, sc_conv, cmp_pe, cmp_w1_k, cmp_w2_k, cmp_w1_v, cmp_w2_v, ret_gn_g, pool_w, pool_scale, w_branch, w_o, norm2_g, w_up, ffn_conv, w_down, final_norm_g):
    batch, seq, d = x.shape
    assert seq == SEQ and d == D_MODEL
    depth = w_in.shape[0]
    small = dict(norm1_g=norm1_g, b_gate=b_gate, sc_conv=sc_conv, cmp_pe=cmp_pe, cmp_w1_k=cmp_w1_k,
                 cmp_w2_k=cmp_w2_k, cmp_w1_v=cmp_w1_v, cmp_w2_v=cmp_w2_v, ret_gn_g=ret_gn_g, pool_scale=pool_scale,
                 norm2_g=norm2_g, ffn_conv=ffn_conv)
    n_main = NG_OFF + NG_COLS
    wts = dict(
        main=jnp.concatenate([w_in[:, :, :n_main], jnp.zeros((depth, D_MODEL, NG_PAD - NG_COLS), w_in.dtype),
                              w_in[:, :, n_main:GATE_OFF_ORIG]], axis=2).astype(BF16),
        gate=w_in[:, :, GATE_OFF_ORIG:].astype(BF16), pool=pool_w.astype(BF16), branch=w_branch.astype(BF16),
        o=w_o.astype(BF16), up=w_up.astype(BF16), down=w_down.astype(BF16))
    pos = jnp.arange(SEQ)
    cmp_end = jnp.asarray(np.arange(N_CMP_PAD) * CMP_STRIDE + CMP_BLOCK - 1)
    tabs = dict(tok=_nsa_rope_tables(pos) + _ret_rope_tables(pos), cmp=_nsa_rope_tables(cmp_end),
                ret=_retention_decay_tables(), ovt=_overlap_t(), onehot=_block_onehot())
    x2 = x.reshape(batch * seq, d)
    for l in range(depth):
        p = {k: v[l] for k, v in small.items()}
        x2 = _layer(x2, batch, l, p, wts, tabs, final_norm_g, final_norm=(l == depth - 1))
    return x2.reshape(batch, seq, d)
```

```python
import functools

import numpy as np
import jax
import jax.numpy as jnp
from jax import lax
from jax.experimental import pallas as pl
from jax.experimental.pallas import tpu as pltpu

F32 = jnp.float32
BF16 = jnp.bfloat16

D_MODEL = 1024
SEQ = 2048
BRANCH = 512
N_BRANCH = 4
CONV_TAPS = 3
NSA_HEADS = 8
NSA_GROUPS = 2
NSA_HPG = NSA_HEADS // NSA_GROUPS
NSA_DH = BRANCH // NSA_HEADS
CMP_BLOCK = 32
CMP_STRIDE = 16
N_CMP = (SEQ - CMP_BLOCK) // CMP_STRIDE + 1
N_CMP_PAD = SEQ // CMP_STRIDE
SEL_BLOCK = 64
N_SEL = SEQ // SEL_BLOCK
SEL_TOP_N = 16
WINDOW = 512
ROPE_THETA = 500000.0
ROPE_DIM = NSA_DH // 4
FORCE_SCORE = 1.0e4
RET_HEADS = 4
RET_DK = BRANCH // RET_HEADS
RET_CHUNK = 128
RET_THETA = 10000.0
POOL_WINDOWS = (2, 4, 8, 16)
POOL_GROUP_DIM = BRANCH // len(POOL_WINDOWS)
D_FF = 2816
EPS = 1e-6
NEG_INF = -1e30
LOG2_E = 1.4426950408889634
Q_SCALE = NSA_DH ** -0.5 * LOG2_E

SC_OFF = 0
Q_OFF = 3 * BRANCH
KV_OFF = Q_OFF + BRANCH
NG_OFF = KV_OFF + 3 * 2 * NSA_GROUPS * NSA_DH
NG_COLS = 3 * NSA_HEADS
NG_PAD = 128
RET_OFF = NG_OFF + NG_PAD
POOL_OFF = RET_OFF + 4 * BRANCH
MAIN_COLS = POOL_OFF + BRANCH
GATE_OFF_ORIG = NG_OFF + NG_COLS + 4 * BRANCH + BRANCH

TOK_TILE = 512
TILES_PER_SEQ = SEQ // TOK_TILE
CONV_HALO = 8
POOL_HALO = 16
ATT_TQ = 256
ATT_TK = 256
KV_LANES = 4 * NSA_GROUPS * 128
CMP_TQ = 2048
VMEM_LIMIT = 56 * 1024 * 1024


def _const_spec(shape):
    n = len(shape)
    return pl.BlockSpec(shape, lambda *_: (0,) * n, pipeline_mode=pl.Buffered(1))


def _layer_spec(layer, shape):
    n = len(shape)
    return pl.BlockSpec((None,) + tuple(shape), lambda *_: (layer,) + (0,) * n, pipeline_mode=pl.Buffered(1))


def _rms(x, g):
    ms = jnp.mean(x * x, axis=-1, keepdims=True)
    return x * lax.rsqrt(ms + EPS) * g


def _dot(a, b):
    return jnp.dot(a, b, preferred_element_type=F32)


def _dot_nt(a, b):
    return lax.dot_general(a, b, (((1,), (1,)), ((), ())), preferred_element_type=F32)


def _inproj_kernel(x_ref, g_ref, wa_ref, wn_ref, wb_ref, scw_ref, nc_ref, nsa_ref, nsb_ref, rc_ref, rs_ref, oh_ref, pw_ref, ps_ref,
                   ya_ref, q_ref, kc_ref, vc_ref, kv2_ref, ng_ref, rq_ref, rg_ref, yd_ref, ch_buf, u_buf):
    T = TOK_TILE
    i = pl.program_id(0)

    @pl.when(i % TILES_PER_SEQ == 0)
    def _():
        ch_buf[0:CONV_HALO, :] = jnp.zeros((CONV_HALO, BRANCH), F32)
        u_buf[0:POOL_HALO, :] = jnp.zeros((POOL_HALO, BRANCH), F32)

    h = _rms(x_ref[...], g_ref[...]).astype(BF16)

    def proj(off, width):
        if off < NG_OFF:
            return _dot(h, wa_ref[:, off:off + width])
        if off == NG_OFF:
            return _dot(h, wn_ref[...])
        return _dot(h, wb_ref[:, off - RET_OFF:off - RET_OFF + width])

    nc, nsa, nsb = nc_ref[...], nsa_ref[...], nsb_ref[...]
    rc, rs = rc_ref[...], rs_ref[...]
    W = 256

    def rope_nsa(x):
        return x * nc + pltpu.roll(x, 128 - ROPE_DIM // 2, 1) * nsa + pltpu.roll(x, ROPE_DIM // 2, 1) * nsb

    def rope_ret(x):
        return x * rc + pltpu.roll(x, RET_DK // 2, 1) * rs

    def short_conv(c0, zs):
        zb, zc, zh = zs
        cols = slice(c0, c0 + W)
        ch_buf[CONV_HALO:CONV_HALO + T, cols] = zc * zh
        conv = scw_ref[CONV_TAPS - 1:CONV_TAPS, cols] * ch_buf[CONV_HALO:CONV_HALO + T, cols]
        for k in range(1, CONV_TAPS):
            conv = conv + scw_ref[CONV_TAPS - 1 - k:CONV_TAPS - k, cols] * ch_buf[CONV_HALO - k:CONV_HALO - k + T, cols]
        ya_ref[:, cols] = (zb * conv).astype(BF16)
        ch_buf[0:CONV_HALO, cols] = ch_buf[T:T + CONV_HALO, cols]

    def pooling(c0, zs):
        (zu,) = zs
        u_buf[POOL_HALO:POOL_HALO + T, c0:c0 + W] = zu
        pos = (i % TILES_PER_SEQ) * T + lax.broadcasted_iota(jnp.int32, (T, 1), 0)
        for gi in range(c0 // POOL_GROUP_DIM, (c0 + W) // POOL_GROUP_DIM):
            win = POOL_WINDOWS[gi]
            g0, g1 = gi * POOL_GROUP_DIM, (gi + 1) * POOL_GROUP_DIM
            s = u_buf[:, g0:g1]
            shift = 1
            while shift < win:
                s = s + pltpu.roll(s, shift, 0)
                shift *= 2
            cnt = jnp.minimum(pos + 1, win).astype(F32)
            pooled = s[POOL_HALO:, :] / cnt - zu[:, g0 - c0:g1 - c0]
            y = _dot(pooled.astype(BF16), pw_ref[gi]) * ps_ref[:, g0:g1]
            yd_ref[:, g0:g1] = y.astype(BF16)
        u_buf[0:POOL_HALO, c0:c0 + W] = u_buf[T:T + POOL_HALO, c0:c0 + W]

    def attn_q(c0, zs):
        for c in range(W // 128):
            lanes = slice(c0 + c * 128, c0 + (c + 1) * 128)
            q_ref[:, lanes] = (rope_nsa(zs[0][:, c * 128:(c + 1) * 128]) * Q_SCALE).astype(BF16)

    def cmp_kv(c0, zs):
        kc_ref[...] = zs[0][:, 0:128]
        vc_ref[...] = zs[0][:, 128:256]

    def attn_kv(c0, zs):
        branch = c0 // W - 1
        key_fill = oh_ref[...] if branch == 0 else jnp.zeros((T, NSA_DH), BF16)
        pieces = ((rope_nsa(zs[0][:, 0:128]).astype(BF16), key_fill),
                  (zs[0][:, 128:256].astype(BF16), jnp.ones((T, NSA_DH), BF16)))
        for c, (piece, fill) in enumerate(pieces):
            for g in range(NSA_GROUPS):
                lane0 = (2 * (2 * branch + c) + g) * 128
                kv2_ref[:, lane0:lane0 + 128] = jnp.concatenate([piece[:, g * NSA_DH:(g + 1) * NSA_DH], fill], axis=1)

    def attn_gates(c0, zs):
        ng_ref[...] = jax.nn.sigmoid(zs[0])

    def ret_q(c0, zs):
        for c in range(W // 128):
            rq_ref[:, c0 + c * 128:c0 + (c + 1) * 128] = rope_ret(zs[0][:, c * 128:(c + 1) * 128]).astype(BF16)

    def ret_k(c0, zs):
        for c in range(W // 128):
            rq_ref[:, BRANCH + c0 + c * 128:BRANCH + c0 + (c + 1) * 128] = (
                rope_ret(zs[0][:, c * 128:(c + 1) * 128]) * (RET_DK ** -0.5)).astype(BF16)

    def ret_v(c0, zs):
        rq_ref[:, 2 * BRANCH + c0:2 * BRANCH + c0 + W] = zs[0].astype(BF16)

    def ret_gate(c0, zs):
        rg_ref[:, c0:c0 + W] = zs[0]

    stages = []
    for c0 in range(0, BRANCH, W):
        stages.append(((SC_OFF + c0, SC_OFF + BRANCH + c0, SC_OFF + 2 * BRANCH + c0), W, c0, short_conv))
    for c0 in range(0, BRANCH, W):
        stages.append(((POOL_OFF + c0,), W, c0, pooling))
    for c0 in range(0, BRANCH, W):
        stages.append(((Q_OFF + c0,), W, c0, attn_q))
    stages.append(((KV_OFF,), W, 0, cmp_kv))
    stages.append(((KV_OFF + W,), W, W, attn_kv))
    stages.append(((KV_OFF + 2 * W,), W, 2 * W, attn_kv))
    stages.append(((NG_OFF,), NG_PAD, 0, attn_gates))
    for consumer, base in ((ret_q, RET_OFF), (ret_k, RET_OFF + BRANCH), (ret_v, RET_OFF + 2 * BRANCH),
                           (ret_gate, RET_OFF + 3 * BRANCH)):
        for c0 in range(0, BRANCH, W):
            stages.append(((base + c0,), W, c0, consumer))
    pending = None
    for offs, width, c0, consumer in stages:
        zs = tuple(proj(off, width) for off in offs)
        if pending is not None:
            pending[0](pending[1], pending[2])
        pending = (consumer, c0, zs)
    pending[0](pending[1], pending[2])


def _inproj(x2, g, w_a, w_ng, w_b, layer, scw, tabs, onehot, pool_w, pool_scale):
    n = x2.shape[0]
    T = TOK_TILE
    tok = lambda width: pl.BlockSpec((T, width), lambda i: (i, 0))
    tab = pl.BlockSpec((T, 128), lambda i: (i % TILES_PER_SEQ, 0))
    out_shapes = (
        jax.ShapeDtypeStruct((n, BRANCH), BF16),
        jax.ShapeDtypeStruct((n, BRANCH), BF16),
        jax.ShapeDtypeStruct((n, 128), F32),
        jax.ShapeDtypeStruct((n, 128), F32),
        jax.ShapeDtypeStruct((n, KV_LANES), BF16),
        jax.ShapeDtypeStruct((n, NG_PAD), F32),
        jax.ShapeDtypeStruct((n, 3 * BRANCH), BF16),
        jax.ShapeDtypeStruct((n, BRANCH), F32),
        jax.ShapeDtypeStruct((n, BRANCH), BF16),
    )
    return pl.pallas_call(
        _inproj_kernel,
        name="inproj",
        grid=(n // T,),
        in_specs=[tok(D_MODEL), _const_spec((1, D_MODEL)), _layer_spec(layer, (D_MODEL, NG_OFF)),
                  _layer_spec(layer, (D_MODEL, NG_PAD)), _layer_spec(layer, (D_MODEL, MAIN_COLS - RET_OFF)),
                  _const_spec((CONV_TAPS, BRANCH)), tab, tab, tab, tab, tab,
                  pl.BlockSpec((T, NSA_DH), lambda i: (i % TILES_PER_SEQ, 0)),
                  _layer_spec(layer, (len(POOL_WINDOWS), POOL_GROUP_DIM, POOL_GROUP_DIM)), _const_spec((1, BRANCH))],
        out_specs=[tok(BRANCH), tok(BRANCH), tok(128), tok(128), tok(KV_LANES), tok(NG_PAD), tok(3 * BRANCH), tok(BRANCH),
                   tok(BRANCH)],
        out_shape=out_shapes,
        scratch_shapes=[pltpu.VMEM((CONV_HALO + T, BRANCH), F32), pltpu.VMEM((POOL_HALO + T, BRANCH), F32)],
        compiler_params=pltpu.CompilerParams(dimension_semantics=("arbitrary",), vmem_limit_bytes=VMEM_LIMIT),
    )(x2, g, w_a, w_ng, w_b, scw, *tabs, onehot, pool_w, pool_scale)


def _compress_kernel(kc_ref, vc_ref, pe_ref, w1_ref, w1f_ref, w2_ref, cc_ref, csa_ref, csb_ref, out_ref):
    for kv, x_ref in enumerate((kc_ref, vc_ref)):
        acc = jnp.zeros((N_CMP_PAD, 4 * NSA_DH), F32)
        for l in range(CMP_STRIDE):
            x = x_ref[0, pl.ds(l, N_CMP_PAD, stride=CMP_STRIDE), :].astype(BF16)
            acc = acc + _dot(x, w1_ref[kv, l])
        bias = _dot(pe_ref[...], w1f_ref[kv])[0:1, :]
        for g in range(NSA_GROUPS):
            a = acc[:, g * 128:(g + 1) * 128]
            hid = jax.nn.gelu(a + pltpu.roll(pltpu.roll(a, N_CMP_PAD - 1, 0), NSA_DH, 1) + bias)
            y = _dot(hid.astype(BF16), w2_ref[kv])
            if kv == 0:
                y = (y * cc_ref[...] + pltpu.roll(y, 128 - ROPE_DIM // 2, 1) * csa_ref[...]
                     + pltpu.roll(y, ROPE_DIM // 2, 1) * csb_ref[...])
            out_ref[0, kv * NSA_GROUPS + g] = y[:, 0:NSA_DH].astype(BF16)


def _compress(kc3, vc3, pe8, w1blk, w1f, w2p, ctabs):
    b = kc3.shape[0]
    return pl.pallas_call(
        _compress_kernel,
        name="compress",
        grid=(b,),
        in_specs=[pl.BlockSpec((1, SEQ, 128), lambda i: (i, 0, 0)), pl.BlockSpec((1, SEQ, 128), lambda i: (i, 0, 0)),
                  _const_spec((8, CMP_BLOCK * NSA_DH)), _const_spec((2, CMP_STRIDE, 128, 4 * NSA_DH)),
                  _const_spec((2, CMP_BLOCK * NSA_DH, 128)), _const_spec((2, 128, 128)),
                  _const_spec((N_CMP_PAD, 128)), _const_spec((N_CMP_PAD, 128)), _const_spec((N_CMP_PAD, 128))],
        out_specs=pl.BlockSpec((1, 4, N_CMP_PAD, NSA_DH), lambda i: (i, 0, 0, 0)),
        out_shape=jax.ShapeDtypeStruct((b, 4, N_CMP_PAD, NSA_DH), BF16),
        compiler_params=pltpu.CompilerParams(dimension_semantics=("arbitrary",), vmem_limit_bytes=VMEM_LIMIT),
    )(kc3, vc3, pe8, w1blk, w1f, w2p, *ctabs)


def _cmp_attn_kernel(q_ref, kc_ref, vc_ref, ovt_ref, o_ref, sb_ref):
    R = CMP_TQ
    r = pl.program_id(2)
    kc, vc = kc_ref[0, 0], vc_ref[0, 0]
    t_rows = r * R + lax.broadcasted_iota(jnp.int32, (R, N_CMP_PAD), 0)
    blk_end = lax.broadcasted_iota(jnp.int32, (R, N_CMP_PAD), 1) * CMP_STRIDE + (CMP_BLOCK - 1)
    valid = blk_end <= t_rows
    qs = jnp.concatenate([q_ref[0, :, j * NSA_DH:(j + 1) * NSA_DH] for j in range(NSA_HPG)], axis=0)
    sm = jnp.where(valid[None], _dot_nt(qs, kc).reshape(NSA_HPG, R, N_CMP_PAD), NEG_INF)
    e = jnp.exp2(sm - jnp.max(sm, axis=-1, keepdims=True))
    p = jnp.where(valid[None], e, 0.0) / jnp.sum(e, axis=-1, keepdims=True)
    o = _dot(p.reshape(NSA_HPG * R, N_CMP_PAD).astype(BF16), vc)
    for j in range(NSA_HPG):
        o_ref[0, :, j * NSA_DH:(j + 1) * NSA_DH] = o[j * R:(j + 1) * R]
    psum = jnp.sum(p, axis=0)
    imp = lax.dot_general(ovt_ref[...], psum, (((1,), (1,)), ((), ())), preferred_element_type=F32,
                          precision=lax.Precision.HIGHEST)
    cur = (r * R + lax.broadcasted_iota(jnp.int32, (N_SEL, R), 1)) // SEL_BLOCK
    bid = lax.broadcasted_iota(jnp.int32, (N_SEL, R), 0)
    forced = (bid == 0) | (bid == cur) | (bid == cur - 1)
    imp = jnp.where(forced, FORCE_SCORE, jnp.where(bid > cur, -FORCE_SCORE, imp))
    rank = jnp.zeros((N_SEL, R), F32)
    for m in range(N_SEL):
        a = imp[m:m + 1, :]
        before = jnp.where(bid > m, jnp.where(a >= imp, 1.0, 0.0), jnp.where(a > imp, 1.0, 0.0))
        rank = rank + before
    bias_t = jnp.where(rank < SEL_TOP_N, 0.0, NEG_INF)
    bias = bias_t.T
    sb_ref[0, 0] = jnp.concatenate([bias, jnp.zeros((R, NSA_DH - N_SEL), F32)], axis=1).astype(BF16)


def _cmp_attn(q3, kcvc, ovt):
    b = q3.shape[0]
    R = CMP_TQ
    gw = NSA_HPG * NSA_DH
    return pl.pallas_call(
        _cmp_attn_kernel,
        name="cmp_attn",
        grid=(b, NSA_GROUPS, SEQ // R),
        in_specs=[pl.BlockSpec((1, R, gw), lambda i, g, r: (i, r, g)),
                  pl.BlockSpec((1, 1, N_CMP_PAD, NSA_DH), lambda i, g, r: (i, g, 0, 0)),
                  pl.BlockSpec((1, 1, N_CMP_PAD, NSA_DH), lambda i, g, r: (i, NSA_GROUPS + g, 0, 0)),
                  _const_spec((N_SEL, N_CMP_PAD))],
        out_specs=[pl.BlockSpec((1, R, gw), lambda i, g, r: (i, r, g)),
                   pl.BlockSpec((1, 1, R, NSA_DH), lambda i, g, r: (i, g, r, 0))],
        out_shape=(jax.ShapeDtypeStruct((b, SEQ, BRANCH), F32),
                   jax.ShapeDtypeStruct((b, NSA_GROUPS, SEQ, NSA_DH), BF16)),
        compiler_params=pltpu.CompilerParams(dimension_semantics=("arbitrary",) * 3, vmem_limit_bytes=VMEM_LIMIT),
    )(q3, kcvc, kcvc, ovt)


def _sel_win_kernel(q_ref, kv_ref, sb_ref, acc_ref, qa_ref, m_ref):
    TQ, TK = ATT_TQ, ATT_TK
    M = NSA_HPG * TQ
    qi = pl.program_id(1)

    def chunk(rows, c, g):
        return kv_ref[0, rows, (2 * c + g) * 128:(2 * c + g + 1) * 128]

    for g in range(NSA_GROUPS):
        sb = sb_ref[0, g]
        for j in range(NSA_HPG):
            h = g * NSA_HPG + j
            qa_ref[g, j * TQ:(j + 1) * TQ, :] = jnp.concatenate([q_ref[0, :, h * NSA_DH:(h + 1) * NSA_DH], sb], axis=1)

    m_ref[...] = jnp.full(m_ref.shape, NEG_INF, F32)
    acc_ref[...] = jnp.zeros(acc_ref.shape, F32)

    def online(idx, s, v):
        m = m_ref[idx]
        m_new = jnp.maximum(m, jnp.max(s, axis=-1, keepdims=True))
        p = jnp.exp2(s - pltpu.repeat(m_new, s.shape[1] // 128, axis=1))
        acc_ref[0, 0, idx] = jnp.exp2(m - m_new) * acc_ref[0, 0, idx] + _dot(p.astype(BF16), v)
        m_ref[idx] = m_new

    def far_tiles(k0, width):
        rows = pl.ds(pl.multiple_of(k0, TK), width)
        for g in range(NSA_GROUPS):
            online(g, _dot_nt(qa_ref[g], chunk(rows, 0, g)), chunk(rows, 1, g))

    def near_tile(t, carry):
        rows = pl.ds(pl.multiple_of(t * TK, TK), TK)
        dist = (lax.broadcasted_iota(jnp.int32, (TQ, TK), 0) - lax.broadcasted_iota(jnp.int32, (TQ, TK), 1)
                + (qi - t) * TK)
        causal_bias = jnp.where(dist >= 0, 0.0, NEG_INF)
        window_bias = jnp.where(lax.bitcast_convert_type(dist, jnp.uint32) < WINDOW, 0.0, NEG_INF)

        def masked(s, bias):
            return (s.reshape(NSA_HPG, TQ, TK) + bias[None]).reshape(M, TK)

        for g in range(NSA_GROUPS):
            qa = qa_ref[g]
            online(g, masked(_dot_nt(qa, chunk(rows, 0, g)), causal_bias), chunk(rows, 1, g))
            online(NSA_GROUPS + g, masked(_dot_nt(qa, chunk(rows, 2, g)), window_bias), chunk(rows, 3, g))
        return carry

    n_far = jnp.maximum(qi - WINDOW // TK, 0)

    def far_pair(i, carry):
        far_tiles(i * (2 * TK), 2 * TK)
        return carry

    lax.fori_loop(0, n_far // 2, far_pair, 0)

    @pl.when(n_far % 2 == 1)
    def _():
        far_tiles((n_far - 1) * TK, TK)

    lax.fori_loop(n_far, qi + 1, near_tile, 0)


def _sel_win(q3, kv3, selbias):
    b = q3.shape[0]
    TQ = ATT_TQ
    M = NSA_HPG * TQ
    return pl.pallas_call(
        _sel_win_kernel,
        name="sel_win_attn",
        grid=(b, SEQ // TQ),
        in_specs=[pl.BlockSpec((1, TQ, BRANCH), lambda i, t: (i, t, 0)),
                  pl.BlockSpec((1, SEQ, KV_LANES), lambda i, t: (i, 0, 0)),
                  pl.BlockSpec((1, NSA_GROUPS, TQ, NSA_DH), lambda i, t: (i, 0, t, 0))],
        out_specs=pl.BlockSpec((1, 1, 2 * NSA_GROUPS, M, 128), lambda i, t: (i, t, 0, 0, 0)),
        out_shape=jax.ShapeDtypeStruct((b, SEQ // TQ, 2 * NSA_GROUPS, M, 128), F32),
        scratch_shapes=[pltpu.VMEM((NSA_GROUPS, M, 128), BF16), pltpu.VMEM((2 * NSA_GROUPS, M, 128), F32)],
        compiler_params=pltpu.CompilerParams(dimension_semantics=("arbitrary", "arbitrary"),
                                             vmem_limit_bytes=VMEM_LIMIT),
    )(q3, kv3, selbias)


def _retention_kernel(qkv_ref, zg_ref, intra_ref, qd_ref, kd_ref, cd_ref, gn_ref, y_ref, st_ref):
    C = RET_CHUNK
    NC = SEQ // C

    def bdot(a, b, ca, cb):
        return lax.dot_general(a, b, (((ca,), (cb,)), ((0,), (0,))), preferred_element_type=F32)

    for h in range(RET_HEADS):
        lanes = slice(h * RET_DK, (h + 1) * RET_DK)
        q3, k3, v3 = (qkv_ref[0, :, part * BRANCH + h * RET_DK:part * BRANCH + (h + 1) * RET_DK].reshape(NC, C, RET_DK)
                      for part in range(3))
        scores = bdot(q3, k3, 2, 2) * intra_ref[h][None]
        inner = bdot(scores.astype(BF16), v3, 2, 1)
        kdec = (k3.astype(F32) * kd_ref[h][None]).astype(BF16)
        kv = bdot(kdec, v3, 1, 1)
        cd = cd_ref[h, 0:1, :]
        state = jnp.zeros((RET_DK, RET_DK), F32)
        for c in range(NC):
            st_ref[c] = state.astype(BF16)
            state = state * cd + kv[c]
        qdec = (q3.astype(F32) * qd_ref[h][None]).astype(BF16)
        o = inner + bdot(qdec, st_ref[...], 2, 1)
        mu = jnp.mean(o, axis=-1, keepdims=True)
        d = o - mu
        var = jnp.mean(d * d, axis=-1, keepdims=True)
        o = (d * lax.rsqrt(var + EPS)).reshape(SEQ, RET_DK) * gn_ref[:, lanes]
        y_ref[0, :, lanes] = (o * jax.nn.silu(zg_ref[0, :, lanes])).astype(BF16)


def _retention(rq3, rg3, intra, qd, kd, cd, gn):
    b = rq3.shape[0]
    return pl.pallas_call(
        _retention_kernel,
        name="retention",
        grid=(b,),
        in_specs=[pl.BlockSpec((1, SEQ, 3 * BRANCH), lambda i: (i, 0, 0)),
                  pl.BlockSpec((1, SEQ, BRANCH), lambda i: (i, 0, 0)),
                  _const_spec((RET_HEADS, RET_CHUNK, RET_DK)), _const_spec((RET_HEADS, RET_CHUNK, RET_DK)),
                  _const_spec((RET_HEADS, RET_CHUNK, RET_DK)), _const_spec((RET_HEADS, 8, RET_DK)),
                  _const_spec((1, BRANCH))],
        out_specs=pl.BlockSpec((1, SEQ, BRANCH), lambda i: (i, 0, 0)),
        out_shape=jax.ShapeDtypeStruct((b, SEQ, BRANCH), BF16),
        scratch_shapes=[pltpu.VMEM((SEQ // RET_CHUNK, RET_DK, RET_DK), BF16)],
        compiler_params=pltpu.CompilerParams(dimension_semantics=("arbitrary",), vmem_limit_bytes=VMEM_LIMIT),
    )(rq3, rg3, intra, qd, kd, cd, gn)


def _merge_kernel(x_ref, g_ref, wg_ref, bg_ref, ya_ref, oc_ref, acc_ref, ng_ref, yc_ref, yd_ref, wb_ref, wo_ref, o_ref):
    T = TOK_TILE
    TQ = ATT_TQ
    x = x_ref[...]
    h = _rms(x, g_ref[...]).astype(BF16)

    lane = lax.broadcasted_iota(jnp.int32, (TQ, 128), 1)

    def branch_out(br):
        rows = []
        for part in range(T // TQ):
            pairs = []
            for g in range(NSA_GROUPS):
                halves = []
                for j in range(NSA_HPG):
                    acc = acc_ref[part, 0, br * NSA_GROUPS + g, j * TQ:(j + 1) * TQ, :]
                    rolled = pltpu.roll(acc, NSA_DH, 1)
                    halves.append(acc * (1.0 / rolled) if j % 2 == 0 else rolled * (1.0 / acc))
                pairs += [jnp.where(lane < NSA_DH, halves[2 * c], halves[2 * c + 1]) for c in range(NSA_HPG // 2)]
            rows.append(jnp.concatenate(pairs, axis=1))
        return jnp.concatenate(rows, axis=0)

    def term(n, y):
        gate = jax.nn.sigmoid(_dot(h, wg_ref[:, n * D_MODEL:(n + 1) * D_MODEL])
                              + bg_ref[:, n * D_MODEL:(n + 1) * D_MODEL])
        return _dot(y, wb_ref[n]) * gate

    def head_gates(br):
        return jnp.concatenate([jnp.broadcast_to(ng[:, 3 * hd + br:3 * hd + br + 1], (T, NSA_DH))
                                for hd in range(NSA_HEADS)], axis=1)

    ng = ng_ref[...]
    merged = term(0, ya_ref[...])
    yb = head_gates(0) * oc_ref[...] + head_gates(1) * branch_out(0)
    merged = merged + term(2, yc_ref[...])
    yb = yb + head_gates(2) * branch_out(1)
    merged = merged + term(3, yd_ref[...])
    merged = merged + term(1, yb.astype(BF16))
    o_ref[...] = x + _dot(merged.astype(BF16), wo_ref[...])


def _merge(x2, g, wg, layer, bg, ya, oc, acc, ng, yc, yd, wb, wo):
    n = x2.shape[0]
    T = TOK_TILE
    tok = lambda width: pl.BlockSpec((T, width), lambda i: (i, 0))
    return pl.pallas_call(
        _merge_kernel,
        name="merge",
        grid=(n // T,),
        in_specs=[tok(D_MODEL), _const_spec((1, D_MODEL)), _layer_spec(layer, (D_MODEL, N_BRANCH * D_MODEL)),
                  _const_spec((1, N_BRANCH * D_MODEL)), tok(BRANCH), tok(BRANCH),
                  pl.BlockSpec((T // ATT_TQ, 1) + acc.shape[2:], lambda i: (i, 0, 0, 0, 0)), tok(NG_PAD),
                  tok(BRANCH), tok(BRANCH),
                  _layer_spec(layer, (N_BRANCH, BRANCH, D_MODEL)), _layer_spec(layer, (D_MODEL, D_MODEL))],
        out_specs=tok(D_MODEL),
        out_shape=jax.ShapeDtypeStruct((n, D_MODEL), F32),
        compiler_params=pltpu.CompilerParams(dimension_semantics=("arbitrary",), vmem_limit_bytes=VMEM_LIMIT),
    )(x2, g, wg, bg, ya, oc, acc, ng, yc, yd, wb, wo)


FF_CHUNK = 256


def _mlp_kernel(x_ref, g_ref, wup_ref, cw_ref, wdn_ref, fg_ref, o_ref, ug_buf, act_buf, *, final_norm):
    T = TOK_TILE
    i = pl.program_id(0)

    @pl.when(i % TILES_PER_SEQ == 0)
    def _():
        ug_buf[0:CONV_HALO, :] = jnp.zeros((CONV_HALO, D_FF), F32)

    x = x_ref[...]
    h = _rms(x, g_ref[...]).astype(BF16)
    for c in range(D_FF // FF_CHUNK):
        cols = slice(c * FF_CHUNK, (c + 1) * FF_CHUNK)
        ug_buf[CONV_HALO:CONV_HALO + T, cols] = _dot(h, wup_ref[:, cols])
        conv = cw_ref[CONV_TAPS - 1:CONV_TAPS, cols] * ug_buf[CONV_HALO:CONV_HALO + T, cols]
        for k in range(1, CONV_TAPS):
            conv = conv + cw_ref[CONV_TAPS - 1 - k:CONV_TAPS - k, cols] * ug_buf[CONV_HALO - k:CONV_HALO - k + T, cols]
        val = _dot(h, wup_ref[:, D_FF + c * FF_CHUNK:D_FF + (c + 1) * FF_CHUNK])
        act_buf[:, cols] = (jax.nn.silu(conv) * val).astype(BF16)
    ug_buf[0:CONV_HALO, :] = ug_buf[T:T + CONV_HALO, :]
    y = x + _dot(act_buf[...], wdn_ref[...])
    if final_norm:
        y = _rms(y, fg_ref[...])
    o_ref[...] = y


def _mlp(x2, g, wup, layer, cw, wdn, fg, final_norm):
    n = x2.shape[0]
    T = TOK_TILE
    tok = pl.BlockSpec((T, D_MODEL), lambda i: (i, 0))
    return pl.pallas_call(
        functools.partial(_mlp_kernel, final_norm=final_norm),
        name="mlp",
        grid=(n // T,),
        in_specs=[tok, _const_spec((1, D_MODEL)), _layer_spec(layer, (D_MODEL, 2 * D_FF)), _const_spec((CONV_TAPS, D_FF)),
                  _layer_spec(layer, (D_FF, D_MODEL)), _const_spec((1, D_MODEL))],
        out_specs=tok,
        out_shape=jax.ShapeDtypeStruct((n, D_MODEL), F32),
        scratch_shapes=[pltpu.VMEM((CONV_HALO + T, D_FF), F32), pltpu.VMEM((T, D_FF), BF16)],
        compiler_params=pltpu.CompilerParams(dimension_semantics=("arbitrary",), vmem_limit_bytes=VMEM_LIMIT),
    )(x2, g, wup, cw, wdn, fg)


def _rope_angles(pos, half, theta):
    inv_freq = np.power(np.float32(theta), -np.arange(half, dtype=np.float32) / np.float32(half))
    ang = pos.astype(F32)[:, None] * jnp.asarray(inv_freq)[None, :]
    return jnp.cos(ang), jnp.sin(ang)


def _nsa_rope_tables(pos):
    half = ROPE_DIM // 2
    cos, sin = _rope_angles(pos, half, ROPE_THETA)
    n = pos.shape[0]
    ones = jnp.ones((n, NSA_DH - ROPE_DIM), F32)
    zeros = jnp.zeros((n, NSA_DH - ROPE_DIM), F32)
    zh = jnp.zeros((n, half), F32)
    c = jnp.concatenate([cos, cos, ones], axis=1)
    sa = jnp.concatenate([-sin, zh, zeros], axis=1)
    sb = jnp.concatenate([zh, sin, zeros], axis=1)
    return tuple(jnp.tile(t, (1, 128 // NSA_DH)) for t in (c, sa, sb))


def _ret_rope_tables(pos):
    half = RET_DK // 2
    cos, sin = _rope_angles(pos, half, RET_THETA)
    return jnp.concatenate([cos, cos], axis=1), jnp.concatenate([-sin, sin], axis=1)


def _retention_decay_tables():
    H, C = RET_HEADS, RET_CHUNK
    log_g = np.log1p(-np.exp2(-5.0 - np.arange(H))).astype(np.float32)
    n = np.arange(C, dtype=np.float32)
    diff = n[:, None] - n[None, :]
    intra = np.where(diff >= 0, np.exp(np.maximum(diff, 0.0)[None] * log_g[:, None, None]), 0.0).astype(np.float32)
    q_dec = np.exp((n[None, :] + 1.0) * log_g[:, None]).astype(np.float32)
    k_dec = np.exp((C - 1.0 - n[None, :]) * log_g[:, None]).astype(np.float32)
    c_dec = np.exp(C * log_g).astype(np.float32)
    qd = np.broadcast_to(q_dec[:, :, None], (H, C, RET_DK))
    kd = np.broadcast_to(k_dec[:, :, None], (H, C, RET_DK))
    cd = np.broadcast_to(c_dec[:, None, None], (H, 8, RET_DK))
    return tuple(jnp.asarray(np.ascontiguousarray(t)) for t in (intra, qd, kd, cd))


def _overlap_t():
    cs0 = np.arange(N_CMP_PAD) * CMP_STRIDE
    ss0 = np.arange(N_SEL) * SEL_BLOCK
    ov = np.clip(np.minimum(cs0[:, None] + CMP_BLOCK, ss0[None, :] + SEL_BLOCK)
                 - np.maximum(cs0[:, None], ss0[None, :]), 0, None)
    ov = (ov / CMP_BLOCK).astype(np.float32)
    ov[N_CMP:, :] = 0.0
    return jnp.asarray(np.ascontiguousarray(ov.T))


def _block_onehot():
    e = np.zeros((SEQ, NSA_DH), np.float32)
    e[np.arange(SEQ), np.arange(SEQ) // SEL_BLOCK] = 1.0
    return jnp.asarray(e, dtype=BF16)


def _layer(x2, batch, layer, p, wts, tabs, final_g, final_norm):
    n = x2.shape[0]
    ya, q, kc_in, vc_in, kv2, ng, rq, rg, yd = _inproj(
        x2, p["norm1_g"][None, :], wts["main_a"], wts["main_ng"], wts["main_b"], layer, p["sc_conv"], tabs["tok"], tabs["onehot"], wts["pool"],
        p["pool_scale"][None, :])

    w1 = jnp.stack([p["cmp_w1_k"], p["cmp_w1_v"]])
    wcat = jnp.concatenate([w1[:, :CMP_STRIDE], w1[:, CMP_STRIDE:]], axis=3)
    zero = jnp.zeros_like(wcat)
    w1blk = jnp.concatenate([jnp.concatenate([wcat, zero], axis=3), jnp.concatenate([zero, wcat], axis=3)],
                            axis=2).astype(BF16)
    w1f = w1.reshape(2, CMP_BLOCK * NSA_DH, NSA_DH)
    w1f = jnp.concatenate([w1f, jnp.zeros_like(w1f)], axis=2).astype(BF16)
    pe8 = jnp.broadcast_to(p["cmp_pe"].reshape(1, -1), (8, CMP_BLOCK * NSA_DH)).astype(BF16)
    w2 = jnp.stack([p["cmp_w2_k"], p["cmp_w2_v"]])
    w2p = jnp.pad(w2, ((0, 0), (0, 128 - NSA_DH), (0, 128 - NSA_DH))).astype(BF16)
    kcvc = _compress(kc_in.reshape(batch, SEQ, 128), vc_in.reshape(batch, SEQ, 128), pe8, w1blk, w1f, w2p,
                     tabs["cmp"])

    q3 = q.reshape(batch, SEQ, BRANCH)
    ocmp, selbias = _cmp_attn(q3, kcvc, tabs["ovt"])
    acc = _sel_win(q3, kv2.reshape(batch, SEQ, KV_LANES), selbias)

    yc = _retention(rq.reshape(batch, SEQ, 3 * BRANCH), rg.reshape(batch, SEQ, BRANCH), *tabs["ret"],
                    p["ret_gn_g"][None, :])

    x2 = _merge(x2, p["norm1_g"][None, :], wts["gate"], layer, p["b_gate"][None, :], ya, ocmp.reshape(n, BRANCH),
                acc.reshape((n // ATT_TQ, 1) + acc.shape[2:]), ng, yc.reshape(n, BRANCH), yd,
                wts["branch"], wts["o"])
    return _mlp(x2, p["norm2_g"][None, :], wts["up"], layer, p["ffn_conv"], wts["down"], final_g[None, :], final_norm)


def kernel(x, norm1_g, w_in, b_gate, sc_conv, cmp_pe, cmp_w1_k, cmp_w2_k, cmp_w1_v, cmp_w2_v, ret_gn_g, pool_w, pool_scale, w_branch, w_o, norm2_g, w_up, ffn_conv, w_down, final_norm_g):
    batch, seq, d = x.shape
    assert seq == SEQ and d == D_MODEL
    depth = w_in.shape[0]
    small = dict(norm1_g=norm1_g, b_gate=b_gate, sc_conv=sc_conv, cmp_pe=cmp_pe, cmp_w1_k=cmp_w1_k,
                 cmp_w2_k=cmp_w2_k, cmp_w1_v=cmp_w1_v, cmp_w2_v=cmp_w2_v, ret_gn_g=ret_gn_g, pool_scale=pool_scale,
                 norm2_g=norm2_g, ffn_conv=ffn_conv)
    n_main = NG_OFF + NG_COLS
    wts = dict(
        main_a=w_in[:, :, :NG_OFF].astype(BF16),
        main_ng=jnp.pad(w_in[:, :, NG_OFF:n_main], ((0, 0), (0, 0), (0, NG_PAD - NG_COLS))).astype(BF16),
        main_b=w_in[:, :, n_main:GATE_OFF_ORIG].astype(BF16),
        gate=w_in[:, :, GATE_OFF_ORIG:].astype(BF16), pool=pool_w.astype(BF16), branch=w_branch.astype(BF16),
        o=w_o.astype(BF16), up=w_up.astype(BF16), down=w_down.astype(BF16))
    pos = jnp.arange(SEQ)
    cmp_end = jnp.asarray(np.arange(N_CMP_PAD) * CMP_STRIDE + CMP_BLOCK - 1)
    tabs = dict(tok=_nsa_rope_tables(pos) + _ret_rope_tables(pos), cmp=_nsa_rope_tables(cmp_end),
                ret=_retention_decay_tables(), ovt=_overlap_t(), onehot=_block_onehot())
    x2 = x.reshape(batch * seq, d)
    for l in range(depth):
        p = {k: v[l] for k, v in small.items()}
        x2 = _layer(x2, batch, l, p, wts, tabs, final_norm_g, final_norm=(l == depth - 1))
    return x2.reshape(batch, seq, d)
```

```python
import functools

import numpy as np
import jax
import jax.numpy as jnp
from jax import lax
from jax.experimental import pallas as pl
from jax.experimental.pallas import tpu as pltpu

F32 = jnp.float32
BF16 = jnp.bfloat16

D_MODEL = 1024
SEQ = 2048
BRANCH = 512
N_BRANCH = 4
CONV_TAPS = 3
NSA_HEADS = 8
NSA_GROUPS = 2
NSA_HPG = NSA_HEADS // NSA_GROUPS
NSA_DH = BRANCH // NSA_HEADS
CMP_BLOCK = 32
CMP_STRIDE = 16
N_CMP = (SEQ - CMP_BLOCK) // CMP_STRIDE + 1
N_CMP_PAD = SEQ // CMP_STRIDE
SEL_BLOCK = 64
N_SEL = SEQ // SEL_BLOCK
SEL_TOP_N = 16
WINDOW = 512
ROPE_THETA = 500000.0
ROPE_DIM = NSA_DH // 4
FORCE_SCORE = 1.0e4
RET_HEADS = 4
RET_DK = BRANCH // RET_HEADS
RET_CHUNK = 128
RET_THETA = 10000.0
POOL_WINDOWS = (2, 4, 8, 16)
POOL_GROUP_DIM = BRANCH // len(POOL_WINDOWS)
D_FF = 2816
EPS = 1e-6
NEG_INF = -1e30
LOG2_E = 1.4426950408889634
Q_SCALE = NSA_DH ** -0.5 * LOG2_E

SC_OFF = 0
Q_OFF = 3 * BRANCH
KV_OFF = Q_OFF + BRANCH
NG_OFF = KV_OFF + 3 * 2 * NSA_GROUPS * NSA_DH
NG_COLS = 3 * NSA_HEADS
NG_PAD = 128
RET_OFF = NG_OFF + NG_PAD
POOL_OFF = RET_OFF + 4 * BRANCH
MAIN_COLS = POOL_OFF + BRANCH
GATE_OFF_ORIG = NG_OFF + NG_COLS + 4 * BRANCH + BRANCH

IN_TILE = 1024
MERGE_TILE = 512
MLP_TILE = 1024
CONV_HALO = 8
POOL_HALO = 16
ATT_TQ = 256
ATT_TK = 256
KV_LANES = 4 * NSA_GROUPS * 128
CMP_TQ = 2048
VMEM_LIMIT = 56 * 1024 * 1024


def _const_spec(shape):
    n = len(shape)
    return pl.BlockSpec(shape, lambda *_: (0,) * n, pipeline_mode=pl.Buffered(1))


def _layer_spec(layer, shape):
    n = len(shape)
    return pl.BlockSpec((None,) + tuple(shape), lambda *_: (layer,) + (0,) * n, pipeline_mode=pl.Buffered(1))


def _rms(x, g):
    ms = jnp.mean(x * x, axis=-1, keepdims=True)
    return x * lax.rsqrt(ms + EPS) * g


def _dot(a, b):
    return jnp.dot(a, b, preferred_element_type=F32)


def _dot_nt(a, b):
    return lax.dot_general(a, b, (((1,), (1,)), ((), ())), preferred_element_type=F32)


def _inproj_kernel(x_ref, g_ref, wa_ref, wn_ref, wb_ref, scw_ref, nc_ref, nsa_ref, nsb_ref, rc_ref, rs_ref, oh_ref, pw_ref, ps_ref,
                   ya_ref, q_ref, kc_ref, vc_ref, kv2_ref, ng_ref, rq_ref, rg_ref, yd_ref, ch_buf, u_buf):
    T = IN_TILE
    tiles_per_seq = SEQ // T
    i = pl.program_id(0)

    @pl.when(i % tiles_per_seq == 0)
    def _():
        ch_buf[0:CONV_HALO, :] = jnp.zeros((CONV_HALO, BRANCH), F32)
        u_buf[0:POOL_HALO, :] = jnp.zeros((POOL_HALO, BRANCH), F32)

    h = _rms(x_ref[...], g_ref[...]).astype(BF16)

    def proj(off, width):
        if off < NG_OFF:
            return _dot(h, wa_ref[:, off:off + width])
        if off == NG_OFF:
            return _dot(h, wn_ref[...])
        return _dot(h, wb_ref[:, off - RET_OFF:off - RET_OFF + width])

    nc, nsa, nsb = nc_ref[...], nsa_ref[...], nsb_ref[...]
    rc, rs = rc_ref[...], rs_ref[...]
    W = 256

    def rope_nsa(x):
        return x * nc + pltpu.roll(x, 128 - ROPE_DIM // 2, 1) * nsa + pltpu.roll(x, ROPE_DIM // 2, 1) * nsb

    def rope_ret(x):
        return x * rc + pltpu.roll(x, RET_DK // 2, 1) * rs

    def short_conv(c0, zs):
        zb, zc, zh = zs
        cols = slice(c0, c0 + W)
        ch_buf[CONV_HALO:CONV_HALO + T, cols] = zc * zh
        conv = scw_ref[CONV_TAPS - 1:CONV_TAPS, cols] * ch_buf[CONV_HALO:CONV_HALO + T, cols]
        for k in range(1, CONV_TAPS):
            conv = conv + scw_ref[CONV_TAPS - 1 - k:CONV_TAPS - k, cols] * ch_buf[CONV_HALO - k:CONV_HALO - k + T, cols]
        ya_ref[:, cols] = (zb * conv).astype(BF16)
        ch_buf[0:CONV_HALO, cols] = ch_buf[T:T + CONV_HALO, cols]

    def pooling(c0, zs):
        (zu,) = zs
        u_buf[POOL_HALO:POOL_HALO + T, c0:c0 + W] = zu
        pos = (i % tiles_per_seq) * T + lax.broadcasted_iota(jnp.int32, (T, 1), 0)
        for gi in range(c0 // POOL_GROUP_DIM, (c0 + W) // POOL_GROUP_DIM):
            win = POOL_WINDOWS[gi]
            g0, g1 = gi * POOL_GROUP_DIM, (gi + 1) * POOL_GROUP_DIM
            s = u_buf[:, g0:g1]
            shift = 1
            while shift < win:
                s = s + pltpu.roll(s, shift, 0)
                shift *= 2
            cnt = jnp.minimum(pos + 1, win).astype(F32)
            pooled = s[POOL_HALO:, :] / cnt - zu[:, g0 - c0:g1 - c0]
            y = _dot(pooled.astype(BF16), pw_ref[gi]) * ps_ref[:, g0:g1]
            yd_ref[:, g0:g1] = y.astype(BF16)
        u_buf[0:POOL_HALO, c0:c0 + W] = u_buf[T:T + POOL_HALO, c0:c0 + W]

    def attn_q(c0, zs):
        for c in range(W // 128):
            lanes = slice(c0 + c * 128, c0 + (c + 1) * 128)
            q_ref[:, lanes] = (rope_nsa(zs[0][:, c * 128:(c + 1) * 128]) * Q_SCALE).astype(BF16)

    def cmp_kv(c0, zs):
        kc_ref[...] = zs[0][:, 0:128]
        vc_ref[...] = zs[0][:, 128:256]

    def attn_kv(c0, zs):
        branch = c0 // W - 1
        key_fill = oh_ref[...] if branch == 0 else jnp.zeros((T, NSA_DH), BF16)
        pieces = ((rope_nsa(zs[0][:, 0:128]).astype(BF16), key_fill),
                  (zs[0][:, 128:256].astype(BF16), jnp.ones((T, NSA_DH), BF16)))
        for c, (piece, fill) in enumerate(pieces):
            for g in range(NSA_GROUPS):
                lane0 = (2 * (2 * branch + c) + g) * 128
                kv2_ref[:, lane0:lane0 + 128] = jnp.concatenate([piece[:, g * NSA_DH:(g + 1) * NSA_DH], fill], axis=1)

    def attn_gates(c0, zs):
        ng_ref[...] = jax.nn.sigmoid(zs[0])

    def ret_q(c0, zs):
        for c in range(W // 128):
            rq_ref[:, c0 + c * 128:c0 + (c + 1) * 128] = rope_ret(zs[0][:, c * 128:(c + 1) * 128]).astype(BF16)

    def ret_k(c0, zs):
        for c in range(W // 128):
            rq_ref[:, BRANCH + c0 + c * 128:BRANCH + c0 + (c + 1) * 128] = (
                rope_ret(zs[0][:, c * 128:(c + 1) * 128]) * (RET_DK ** -0.5)).astype(BF16)

    def ret_v(c0, zs):
        rq_ref[:, 2 * BRANCH + c0:2 * BRANCH + c0 + W] = zs[0].astype(BF16)

    def ret_gate(c0, zs):
        rg_ref[:, c0:c0 + W] = zs[0]

    stages = []
    for c0 in range(0, BRANCH, W):
        stages.append(((SC_OFF + c0, SC_OFF + BRANCH + c0, SC_OFF + 2 * BRANCH + c0), W, c0, short_conv))
    for c0 in range(0, BRANCH, W):
        stages.append(((POOL_OFF + c0,), W, c0, pooling))
    for c0 in range(0, BRANCH, W):
        stages.append(((Q_OFF + c0,), W, c0, attn_q))
    stages.append(((KV_OFF,), W, 0, cmp_kv))
    stages.append(((KV_OFF + W,), W, W, attn_kv))
    stages.append(((KV_OFF + 2 * W,), W, 2 * W, attn_kv))
    stages.append(((NG_OFF,), NG_PAD, 0, attn_gates))
    for consumer, base in ((ret_q, RET_OFF), (ret_k, RET_OFF + BRANCH), (ret_v, RET_OFF + 2 * BRANCH),
                           (ret_gate, RET_OFF + 3 * BRANCH)):
        for c0 in range(0, BRANCH, W):
            stages.append(((base + c0,), W, c0, consumer))
    pending = None
    for offs, width, c0, consumer in stages:
        zs = tuple(proj(off, width) for off in offs)
        if pending is not None:
            pending[0](pending[1], pending[2])
        pending = (consumer, c0, zs)
    pending[0](pending[1], pending[2])


def _inproj(x2, g, w_a, w_ng, w_b, layer, scw, tabs, onehot, pool_w, pool_scale):
    n = x2.shape[0]
    T = IN_TILE
    tok = lambda width: pl.BlockSpec((T, width), lambda i: (i, 0))
    tab = pl.BlockSpec((T, 128), lambda i: (i % (SEQ // T), 0))
    out_shapes = (
        jax.ShapeDtypeStruct((n, BRANCH), BF16),
        jax.ShapeDtypeStruct((n, BRANCH), BF16),
        jax.ShapeDtypeStruct((n, 128), F32),
        jax.ShapeDtypeStruct((n, 128), F32),
        jax.ShapeDtypeStruct((n, KV_LANES), BF16),
        jax.ShapeDtypeStruct((n, NG_PAD), F32),
        jax.ShapeDtypeStruct((n, 3 * BRANCH), BF16),
        jax.ShapeDtypeStruct((n, BRANCH), F32),
        jax.ShapeDtypeStruct((n, BRANCH), BF16),
    )
    return pl.pallas_call(
        _inproj_kernel,
        name="inproj",
        grid=(n // T,),
        in_specs=[tok(D_MODEL), _const_spec((1, D_MODEL)), _layer_spec(layer, (D_MODEL, NG_OFF)),
                  _layer_spec(layer, (D_MODEL, NG_PAD)), _layer_spec(layer, (D_MODEL, MAIN_COLS - RET_OFF)),
                  _const_spec((CONV_TAPS, BRANCH)), tab, tab, tab, tab, tab,
                  pl.BlockSpec((T, NSA_DH), lambda i: (i % (SEQ // T), 0)),
                  _layer_spec(layer, (len(POOL_WINDOWS), POOL_GROUP_DIM, POOL_GROUP_DIM)), _const_spec((1, BRANCH))],
        out_specs=[tok(BRANCH), tok(BRANCH), tok(128), tok(128), tok(KV_LANES), tok(NG_PAD), tok(3 * BRANCH), tok(BRANCH),
                   tok(BRANCH)],
        out_shape=out_shapes,
        scratch_shapes=[pltpu.VMEM((CONV_HALO + T, BRANCH), F32), pltpu.VMEM((POOL_HALO + T, BRANCH), F32)],
        compiler_params=pltpu.CompilerParams(dimension_semantics=("arbitrary",), vmem_limit_bytes=VMEM_LIMIT),
    )(x2, g, w_a, w_ng, w_b, scw, *tabs, onehot, pool_w, pool_scale)


def _compress_kernel(kc_ref, vc_ref, pe_ref, w1_ref, w1f_ref, w2_ref, cc_ref, csa_ref, csb_ref, out_ref):
    for kv, x_ref in enumerate((kc_ref, vc_ref)):
        acc = jnp.zeros((N_CMP_PAD, 4 * NSA_DH), F32)
        for l in range(CMP_STRIDE):
            x = x_ref[0, pl.ds(l, N_CMP_PAD, stride=CMP_STRIDE), :].astype(BF16)
            acc = acc + _dot(x, w1_ref[kv, l])
        bias = _dot(pe_ref[...], w1f_ref[kv])[0:1, :]
        for g in range(NSA_GROUPS):
            a = acc[:, g * 128:(g + 1) * 128]
            hid = jax.nn.gelu(a + pltpu.roll(pltpu.roll(a, N_CMP_PAD - 1, 0), NSA_DH, 1) + bias)
            y = _dot(hid.astype(BF16), w2_ref[kv])
            if kv == 0:
                y = (y * cc_ref[...] + pltpu.roll(y, 128 - ROPE_DIM // 2, 1) * csa_ref[...]
                     + pltpu.roll(y, ROPE_DIM // 2, 1) * csb_ref[...])
            out_ref[0, kv * NSA_GROUPS + g] = y[:, 0:NSA_DH].astype(BF16)


def _compress(kc3, vc3, pe8, w1blk, w1f, w2p, ctabs):
    b = kc3.shape[0]
    return pl.pallas_call(
        _compress_kernel,
        name="compress",
        grid=(b,),
        in_specs=[pl.BlockSpec((1, SEQ, 128), lambda i: (i, 0, 0)), pl.BlockSpec((1, SEQ, 128), lambda i: (i, 0, 0)),
                  _const_spec((8, CMP_BLOCK * NSA_DH)), _const_spec((2, CMP_STRIDE, 128, 4 * NSA_DH)),
                  _const_spec((2, CMP_BLOCK * NSA_DH, 128)), _const_spec((2, 128, 128)),
                  _const_spec((N_CMP_PAD, 128)), _const_spec((N_CMP_PAD, 128)), _const_spec((N_CMP_PAD, 128))],
        out_specs=pl.BlockSpec((1, 4, N_CMP_PAD, NSA_DH), lambda i: (i, 0, 0, 0)),
        out_shape=jax.ShapeDtypeStruct((b, 4, N_CMP_PAD, NSA_DH), BF16),
        compiler_params=pltpu.CompilerParams(dimension_semantics=("arbitrary",), vmem_limit_bytes=VMEM_LIMIT),
    )(kc3, vc3, pe8, w1blk, w1f, w2p, *ctabs)


def _cmp_attn_kernel(q_ref, kc_ref, vc_ref, ovt_ref, o_ref, sb_ref):
    R = CMP_TQ
    r = pl.program_id(2)
    kc, vc = kc_ref[0, 0], vc_ref[0, 0]
    t_rows = r * R + lax.broadcasted_iota(jnp.int32, (R, N_CMP_PAD), 0)
    blk_end = lax.broadcasted_iota(jnp.int32, (R, N_CMP_PAD), 1) * CMP_STRIDE + (CMP_BLOCK - 1)
    valid = blk_end <= t_rows
    qs = jnp.concatenate([q_ref[0, :, j * NSA_DH:(j + 1) * NSA_DH] for j in range(NSA_HPG)], axis=0)
    sm = jnp.where(valid[None], _dot_nt(qs, kc).reshape(NSA_HPG, R, N_CMP_PAD), NEG_INF)
    e = jnp.exp2(sm - jnp.max(sm, axis=-1, keepdims=True))
    p = jnp.where(valid[None], e, 0.0) / jnp.sum(e, axis=-1, keepdims=True)
    o = _dot(p.reshape(NSA_HPG * R, N_CMP_PAD).astype(BF16), vc)
    for j in range(NSA_HPG):
        o_ref[0, :, j * NSA_DH:(j + 1) * NSA_DH] = o[j * R:(j + 1) * R]
    psum = jnp.sum(p, axis=0)
    imp = lax.dot_general(ovt_ref[...], psum, (((1,), (1,)), ((), ())), preferred_element_type=F32,
                          precision=lax.Precision.HIGHEST)
    cur = (r * R + lax.broadcasted_iota(jnp.int32, (N_SEL, R), 1)) // SEL_BLOCK
    bid = lax.broadcasted_iota(jnp.int32, (N_SEL, R), 0)
    forced = (bid == 0) | (bid == cur) | (bid == cur - 1)
    imp = jnp.where(forced, FORCE_SCORE, jnp.where(bid > cur, -FORCE_SCORE, imp))
    rank = jnp.zeros((N_SEL, R), F32)
    for m in range(N_SEL):
        a = imp[m:m + 1, :]
        before = jnp.where(bid > m, jnp.where(a >= imp, 1.0, 0.0), jnp.where(a > imp, 1.0, 0.0))
        rank = rank + before
    bias_t = jnp.where(rank < SEL_TOP_N, 0.0, NEG_INF)
    bias = bias_t.T
    sb_ref[0, 0] = jnp.concatenate([bias, jnp.zeros((R, NSA_DH - N_SEL), F32)], axis=1).astype(BF16)


def _cmp_attn(q3, kcvc, ovt):
    b = q3.shape[0]
    R = CMP_TQ
    gw = NSA_HPG * NSA_DH
    return pl.pallas_call(
        _cmp_attn_kernel,
        name="cmp_attn",
        grid=(b, NSA_GROUPS, SEQ // R),
        in_specs=[pl.BlockSpec((1, R, gw), lambda i, g, r: (i, r, g)),
                  pl.BlockSpec((1, 1, N_CMP_PAD, NSA_DH), lambda i, g, r: (i, g, 0, 0)),
                  pl.BlockSpec((1, 1, N_CMP_PAD, NSA_DH), lambda i, g, r: (i, NSA_GROUPS + g, 0, 0)),
                  _const_spec((N_SEL, N_CMP_PAD))],
        out_specs=[pl.BlockSpec((1, R, gw), lambda i, g, r: (i, r, g)),
                   pl.BlockSpec((1, 1, R, NSA_DH), lambda i, g, r: (i, g, r, 0))],
        out_shape=(jax.ShapeDtypeStruct((b, SEQ, BRANCH), F32),
                   jax.ShapeDtypeStruct((b, NSA_GROUPS, SEQ, NSA_DH), BF16)),
        compiler_params=pltpu.CompilerParams(dimension_semantics=("arbitrary",) * 3, vmem_limit_bytes=VMEM_LIMIT),
    )(q3, kcvc, kcvc, ovt)


def _sel_win_kernel(q_ref, kv_ref, sb_ref, acc_ref, qa_ref, m_ref):
    TQ, TK = ATT_TQ, ATT_TK
    M = NSA_HPG * TQ
    qi = pl.program_id(1)

    def chunk(rows, c, g):
        return kv_ref[0, rows, (2 * c + g) * 128:(2 * c + g + 1) * 128]

    for g in range(NSA_GROUPS):
        sb = sb_ref[0, g]
        for j in range(NSA_HPG):
            h = g * NSA_HPG + j
            qa_ref[g, j * TQ:(j + 1) * TQ, :] = jnp.concatenate([q_ref[0, :, h * NSA_DH:(h + 1) * NSA_DH], sb], axis=1)

    m_ref[...] = jnp.full(m_ref.shape, NEG_INF, F32)
    acc_ref[...] = jnp.zeros(acc_ref.shape, F32)

    def online(idx, s, v):
        m = m_ref[idx]
        m_new = jnp.maximum(m, jnp.max(s, axis=-1, keepdims=True))
        p = jnp.exp2(s - pltpu.repeat(m_new, s.shape[1] // 128, axis=1))
        acc_ref[0, 0, idx] = jnp.exp2(m - m_new) * acc_ref[0, 0, idx] + _dot(p.astype(BF16), v)
        m_ref[idx] = m_new

    def far_tiles(k0, width):
        rows = pl.ds(pl.multiple_of(k0, TK), width)
        for g in range(NSA_GROUPS):
            online(g, _dot_nt(qa_ref[g], chunk(rows, 0, g)), chunk(rows, 1, g))

    def near_tile(t):
        rows = pl.ds(pl.multiple_of(t * TK, TK), TK)
        dist = (lax.broadcasted_iota(jnp.int32, (TQ, TK), 0) - lax.broadcasted_iota(jnp.int32, (TQ, TK), 1)
                + (qi - t) * TK)
        causal_bias = jnp.where(dist >= 0, 0.0, NEG_INF)
        window_bias = jnp.where(lax.bitcast_convert_type(dist, jnp.uint32) < WINDOW, 0.0, NEG_INF)

        def masked(s, bias):
            return (s.reshape(NSA_HPG, TQ, TK) + bias[None]).reshape(M, TK)

        for g in range(NSA_GROUPS):
            qa = qa_ref[g]
            online(g, masked(_dot_nt(qa, chunk(rows, 0, g)), causal_bias), chunk(rows, 1, g))
            online(NSA_GROUPS + g, masked(_dot_nt(qa, chunk(rows, 2, g)), window_bias), chunk(rows, 3, g))

    n_far = jnp.maximum(qi - WINDOW // TK, 0)

    def far_pair(i, carry):
        far_tiles(i * (2 * TK), 2 * TK)
        return carry

    lax.fori_loop(0, n_far // 2, far_pair, 0)

    @pl.when(n_far % 2 == 1)
    def _():
        far_tiles((n_far - 1) * TK, TK)

    max_near = WINDOW // TK + 1
    for count in range(1, max_near + 1):
        @pl.when((qi == count - 1) if count < max_near else (qi >= count - 1))
        def _(count=count):
            for back in range(count - 1, -1, -1):
                near_tile(qi - back)


def _sel_win(q3, kv3, selbias):
    b = q3.shape[0]
    TQ = ATT_TQ
    M = NSA_HPG * TQ
    return pl.pallas_call(
        _sel_win_kernel,
        name="sel_win_attn",
        grid=(b, SEQ // TQ),
        in_specs=[pl.BlockSpec((1, TQ, BRANCH), lambda i, t: (i, t, 0)),
                  pl.BlockSpec((1, SEQ, KV_LANES), lambda i, t: (i, 0, 0)),
                  pl.BlockSpec((1, NSA_GROUPS, TQ, NSA_DH), lambda i, t: (i, 0, t, 0))],
        out_specs=pl.BlockSpec((1, 1, 2 * NSA_GROUPS, M, 128), lambda i, t: (i, t, 0, 0, 0)),
        out_shape=jax.ShapeDtypeStruct((b, SEQ // TQ, 2 * NSA_GROUPS, M, 128), F32),
        scratch_shapes=[pltpu.VMEM((NSA_GROUPS, M, 128), BF16), pltpu.VMEM((2 * NSA_GROUPS, M, 128), F32)],
        compiler_params=pltpu.CompilerParams(dimension_semantics=("arbitrary", "arbitrary"),
                                             vmem_limit_bytes=VMEM_LIMIT),
    )(q3, kv3, selbias)


def _retention_kernel(qkv_ref, zg_ref, intra_ref, qd_ref, kd_ref, cd_ref, gn_ref, y_ref, st_ref):
    C = RET_CHUNK
    NC = SEQ // C

    def bdot(a, b, ca, cb):
        return lax.dot_general(a, b, (((ca,), (cb,)), ((0,), (0,))), preferred_element_type=F32)

    for h in range(RET_HEADS):
        lanes = slice(h * RET_DK, (h + 1) * RET_DK)
        q3, k3, v3 = (qkv_ref[0, :, part * BRANCH + h * RET_DK:part * BRANCH + (h + 1) * RET_DK].reshape(NC, C, RET_DK)
                      for part in range(3))
        scores = bdot(q3, k3, 2, 2) * intra_ref[h][None]
        inner = bdot(scores.astype(BF16), v3, 2, 1)
        kdec = (k3.astype(F32) * kd_ref[h][None]).astype(BF16)
        kv = bdot(kdec, v3, 1, 1)
        cd = cd_ref[h, 0:1, :]
        state = jnp.zeros((RET_DK, RET_DK), F32)
        for c in range(NC):
            st_ref[c] = state.astype(BF16)
            state = state * cd + kv[c]
        qdec = (q3.astype(F32) * qd_ref[h][None]).astype(BF16)
        o = inner + bdot(qdec, st_ref[...], 2, 1)
        mu = jnp.mean(o, axis=-1, keepdims=True)
        d = o - mu
        var = jnp.mean(d * d, axis=-1, keepdims=True)
        o = (d * lax.rsqrt(var + EPS)).reshape(SEQ, RET_DK) * gn_ref[:, lanes]
        y_ref[0, :, lanes] = (o * jax.nn.silu(zg_ref[0, :, lanes])).astype(BF16)


def _retention(rq3, rg3, intra, qd, kd, cd, gn):
    b = rq3.shape[0]
    return pl.pallas_call(
        _retention_kernel,
        name="retention",
        grid=(b,),
        in_specs=[pl.BlockSpec((1, SEQ, 3 * BRANCH), lambda i: (i, 0, 0)),
                  pl.BlockSpec((1, SEQ, BRANCH), lambda i: (i, 0, 0)),
                  _const_spec((RET_HEADS, RET_CHUNK, RET_DK)), _const_spec((RET_HEADS, RET_CHUNK, RET_DK)),
                  _const_spec((RET_HEADS, RET_CHUNK, RET_DK)), _const_spec((RET_HEADS, 8, RET_DK)),
                  _const_spec((1, BRANCH))],
        out_specs=pl.BlockSpec((1, SEQ, BRANCH), lambda i: (i, 0, 0)),
        out_shape=jax.ShapeDtypeStruct((b, SEQ, BRANCH), BF16),
        scratch_shapes=[pltpu.VMEM((SEQ // RET_CHUNK, RET_DK, RET_DK), BF16)],
        compiler_params=pltpu.CompilerParams(dimension_semantics=("arbitrary",), vmem_limit_bytes=VMEM_LIMIT),
    )(rq3, rg3, intra, qd, kd, cd, gn)


def _merge_kernel(x_ref, g_ref, wg_ref, bg_ref, ya_ref, oc_ref, acc_ref, ng_ref, yc_ref, yd_ref, wb_ref, wo_ref, o_ref):
    T = MERGE_TILE
    TQ = ATT_TQ
    x = x_ref[...]
    h = _rms(x, g_ref[...]).astype(BF16)

    lane = lax.broadcasted_iota(jnp.int32, (TQ, 128), 1)

    def branch_out(br):
        rows = []
        for part in range(T // TQ):
            pairs = []
            for g in range(NSA_GROUPS):
                halves = []
                for j in range(NSA_HPG):
                    acc = acc_ref[part, 0, br * NSA_GROUPS + g, j * TQ:(j + 1) * TQ, :]
                    rolled = pltpu.roll(acc, NSA_DH, 1)
                    halves.append(acc * (1.0 / rolled) if j % 2 == 0 else rolled * (1.0 / acc))
                pairs += [jnp.where(lane < NSA_DH, halves[2 * c], halves[2 * c + 1]) for c in range(NSA_HPG // 2)]
            rows.append(jnp.concatenate(pairs, axis=1))
        return jnp.concatenate(rows, axis=0)

    def term(n, y):
        gate = jax.nn.sigmoid(_dot(h, wg_ref[:, n * D_MODEL:(n + 1) * D_MODEL])
                              + bg_ref[:, n * D_MODEL:(n + 1) * D_MODEL])
        return _dot(y, wb_ref[n]) * gate

    def head_gates(br):
        return jnp.concatenate([jnp.broadcast_to(ng[:, 3 * hd + br:3 * hd + br + 1], (T, NSA_DH))
                                for hd in range(NSA_HEADS)], axis=1)

    ng = ng_ref[...]
    merged = term(0, ya_ref[...])
    yb = head_gates(0) * oc_ref[...] + head_gates(1) * branch_out(0)
    merged = merged + term(2, yc_ref[...])
    yb = yb + head_gates(2) * branch_out(1)
    merged = merged + term(3, yd_ref[...])
    merged = merged + term(1, yb.astype(BF16))
    o_ref[...] = x + _dot(merged.astype(BF16), wo_ref[...])


def _merge(x2, g, wg, layer, bg, ya, oc, acc, ng, yc, yd, wb, wo):
    n = x2.shape[0]
    T = MERGE_TILE
    tok = lambda width: pl.BlockSpec((T, width), lambda i: (i, 0))
    return pl.pallas_call(
        _merge_kernel,
        name="merge",
        grid=(n // T,),
        in_specs=[tok(D_MODEL), _const_spec((1, D_MODEL)), _layer_spec(layer, (D_MODEL, N_BRANCH * D_MODEL)),
                  _const_spec((1, N_BRANCH * D_MODEL)), tok(BRANCH), tok(BRANCH),
                  pl.BlockSpec((T // ATT_TQ, 1) + acc.shape[2:], lambda i: (i, 0, 0, 0, 0)), tok(NG_PAD),
                  tok(BRANCH), tok(BRANCH),
                  _layer_spec(layer, (N_BRANCH, BRANCH, D_MODEL)), _layer_spec(layer, (D_MODEL, D_MODEL))],
        out_specs=tok(D_MODEL),
        out_shape=jax.ShapeDtypeStruct((n, D_MODEL), F32),
        compiler_params=pltpu.CompilerParams(dimension_semantics=("arbitrary",), vmem_limit_bytes=VMEM_LIMIT),
    )(x2, g, wg, bg, ya, oc, acc, ng, yc, yd, wb, wo)


FF_CHUNK = 256


def _mlp_kernel(x_ref, g_ref, wup_ref, cw_ref, wdn_ref, fg_ref, o_ref, ug_buf, act_buf, *, final_norm):
    T = MLP_TILE
    i = pl.program_id(0)

    @pl.when(i % (SEQ // T) == 0)
    def _():
        ug_buf[0:CONV_HALO, :] = jnp.zeros((CONV_HALO, D_FF), F32)

    x = x_ref[...]
    h = _rms(x, g_ref[...]).astype(BF16)
    for c in range(D_FF // FF_CHUNK):
        cols = slice(c * FF_CHUNK, (c + 1) * FF_CHUNK)
        ug_buf[CONV_HALO:CONV_HALO + T, cols] = _dot(h, wup_ref[:, cols])
        conv = cw_ref[CONV_TAPS - 1:CONV_TAPS, cols] * ug_buf[CONV_HALO:CONV_HALO + T, cols]
        for k in range(1, CONV_TAPS):
            conv = conv + cw_ref[CONV_TAPS - 1 - k:CONV_TAPS - k, cols] * ug_buf[CONV_HALO - k:CONV_HALO - k + T, cols]
        val = _dot(h, wup_ref[:, D_FF + c * FF_CHUNK:D_FF + (c + 1) * FF_CHUNK])
        act_buf[:, cols] = (jax.nn.silu(conv) * val).astype(BF16)
    ug_buf[0:CONV_HALO, :] = ug_buf[T:T + CONV_HALO, :]
    y = x + _dot(act_buf[...], wdn_ref[...])
    if final_norm:
        y = _rms(y, fg_ref[...])
    o_ref[...] = y


def _mlp(x2, g, wup, layer, cw, wdn, fg, final_norm):
    n = x2.shape[0]
    T = MLP_TILE
    tok = pl.BlockSpec((T, D_MODEL), lambda i: (i, 0))
    return pl.pallas_call(
        functools.partial(_mlp_kernel, final_norm=final_norm),
        name="mlp",
        grid=(n // T,),
        in_specs=[tok, _const_spec((1, D_MODEL)), _layer_spec(layer, (D_MODEL, 2 * D_FF)), _const_spec((CONV_TAPS, D_FF)),
                  _layer_spec(layer, (D_FF, D_MODEL)), _const_spec((1, D_MODEL))],
        out_specs=tok,
        out_shape=jax.ShapeDtypeStruct((n, D_MODEL), F32),
        scratch_shapes=[pltpu.VMEM((CONV_HALO + T, D_FF), F32), pltpu.VMEM((T, D_FF), BF16)],
        compiler_params=pltpu.CompilerParams(dimension_semantics=("arbitrary",), vmem_limit_bytes=VMEM_LIMIT),
    )(x2, g, wup, cw, wdn, fg)


def _rope_angles(pos, half, theta):
    inv_freq = np.power(np.float32(theta), -np.arange(half, dtype=np.float32) / np.float32(half))
    ang = pos.astype(F32)[:, None] * jnp.asarray(inv_freq)[None, :]
    return jnp.cos(ang), jnp.sin(ang)


def _nsa_rope_tables(pos):
    half = ROPE_DIM // 2
    cos, sin = _rope_angles(pos, half, ROPE_THETA)
    n = pos.shape[0]
    ones = jnp.ones((n, NSA_DH - ROPE_DIM), F32)
    zeros = jnp.zeros((n, NSA_DH - ROPE_DIM), F32)
    zh = jnp.zeros((n, half), F32)
    c = jnp.concatenate([cos, cos, ones], axis=1)
    sa = jnp.concatenate([-sin, zh, zeros], axis=1)
    sb = jnp.concatenate([zh, sin, zeros], axis=1)
    return tuple(jnp.tile(t, (1, 128 // NSA_DH)) for t in (c, sa, sb))


def _ret_rope_tables(pos):
    half = RET_DK // 2
    cos, sin = _rope_angles(pos, half, RET_THETA)
    return jnp.concatenate([cos, cos], axis=1), jnp.concatenate([-sin, sin], axis=1)


def _retention_decay_tables():
    H, C = RET_HEADS, RET_CHUNK
    log_g = np.log1p(-np.exp2(-5.0 - np.arange(H))).astype(np.float32)
    n = np.arange(C, dtype=np.float32)
    diff = n[:, None] - n[None, :]
    intra = np.where(diff >= 0, np.exp(np.maximum(diff, 0.0)[None] * log_g[:, None, None]), 0.0).astype(np.float32)
    q_dec = np.exp((n[None, :] + 1.0) * log_g[:, None]).astype(np.float32)
    k_dec = np.exp((C - 1.0 - n[None, :]) * log_g[:, None]).astype(np.float32)
    c_dec = np.exp(C * log_g).astype(np.float32)
    qd = np.broadcast_to(q_dec[:, :, None], (H, C, RET_DK))
    kd = np.broadcast_to(k_dec[:, :, None], (H, C, RET_DK))
    cd = np.broadcast_to(c_dec[:, None, None], (H, 8, RET_DK))
    return tuple(jnp.asarray(np.ascontiguousarray(t)) for t in (intra, qd, kd, cd))


def _overlap_t():
    cs0 = np.arange(N_CMP_PAD) * CMP_STRIDE
    ss0 = np.arange(N_SEL) * SEL_BLOCK
    ov = np.clip(np.minimum(cs0[:, None] + CMP_BLOCK, ss0[None, :] + SEL_BLOCK)
                 - np.maximum(cs0[:, None], ss0[None, :]), 0, None)
    ov = (ov / CMP_BLOCK).astype(np.float32)
    ov[N_CMP:, :] = 0.0
    return jnp.asarray(np.ascontiguousarray(ov.T))


def _block_onehot():
    e = np.zeros((SEQ, NSA_DH), np.float32)
    e[np.arange(SEQ), np.arange(SEQ) // SEL_BLOCK] = 1.0
    return jnp.asarray(e, dtype=BF16)


def _layer(x2, batch, layer, p, wts, tabs, final_g, final_norm):
    n = x2.shape[0]
    ya, q, kc_in, vc_in, kv2, ng, rq, rg, yd = _inproj(
        x2, p["norm1_g"][None, :], wts["main_a"], wts["main_ng"], wts["main_b"], layer, p["sc_conv"], tabs["tok"], tabs["onehot"], wts["pool"],
        p["pool_scale"][None, :])

    w1 = jnp.stack([p["cmp_w1_k"], p["cmp_w1_v"]])
    wcat = jnp.concatenate([w1[:, :CMP_STRIDE], w1[:, CMP_STRIDE:]], axis=3)
    zero = jnp.zeros_like(wcat)
    w1blk = jnp.concatenate([jnp.concatenate([wcat, zero], axis=3), jnp.concatenate([zero, wcat], axis=3)],
                            axis=2).astype(BF16)
    w1f = w1.reshape(2, CMP_BLOCK * NSA_DH, NSA_DH)
    w1f = jnp.concatenate([w1f, jnp.zeros_like(w1f)], axis=2).astype(BF16)
    pe8 = jnp.broadcast_to(p["cmp_pe"].reshape(1, -1), (8, CMP_BLOCK * NSA_DH)).astype(BF16)
    w2 = jnp.stack([p["cmp_w2_k"], p["cmp_w2_v"]])
    w2p = jnp.pad(w2, ((0, 0), (0, 128 - NSA_DH), (0, 128 - NSA_DH))).astype(BF16)
    kcvc = _compress(kc_in.reshape(batch, SEQ, 128), vc_in.reshape(batch, SEQ, 128), pe8, w1blk, w1f, w2p,
                     tabs["cmp"])

    q3 = q.reshape(batch, SEQ, BRANCH)
    ocmp, selbias = _cmp_attn(q3, kcvc, tabs["ovt"])
    acc = _sel_win(q3, kv2.reshape(batch, SEQ, KV_LANES), selbias)

    yc = _retention(rq.reshape(batch, SEQ, 3 * BRANCH), rg.reshape(batch, SEQ, BRANCH), *tabs["ret"],
                    p["ret_gn_g"][None, :])

    x2 = _merge(x2, p["norm1_g"][None, :], wts["gate"], layer, p["b_gate"][None, :], ya, ocmp.reshape(n, BRANCH),
                acc.reshape((n // ATT_TQ, 1) + acc.shape[2:]), ng, yc.reshape(n, BRANCH), yd,
                wts["branch"], wts["o"])
    return _mlp(x2, p["norm2_g"][None, :], wts["up"], layer, p["ffn_conv"], wts["down"], final_g[None, :], final_norm)


def kernel(x, norm1_g, w_in, b_gate, sc_conv, cmp_pe, cmp_w1_k, cmp_w2_k, cmp_w1_v, cmp_w2_v, ret_gn_g, pool_w, pool_scale, w_branch, w_o, norm2_g, w_up, ffn_conv, w_down, final_norm_g):
    batch, seq, d = x.shape
    assert seq == SEQ and d == D_MODEL
    depth = w_in.shape[0]
    small = dict(norm1_g=norm1_g, b_gate=b_gate, sc_conv=sc_conv, cmp_pe=cmp_pe, cmp_w1_k=cmp_w1_k,
                 cmp_w2_k=cmp_w2_k, cmp_w1_v=cmp_w1_v, cmp_w2_v=cmp_w2_v, ret_gn_g=ret_gn_g, pool_scale=pool_scale,
                 norm2_g=norm2_g, ffn_conv=ffn_conv)
    n_main = NG_OFF + NG_COLS
    w_in16 = w_in.astype(BF16)
    wts = dict(
        main_a=w_in16[:, :, :NG_OFF],
        main_ng=jnp.pad(w_in16[:, :, NG_OFF:n_main], ((0, 0), (0, 0), (0, NG_PAD - NG_COLS))),
        main_b=w_in16[:, :, n_main:GATE_OFF_ORIG],
        gate=w_in16[:, :, GATE_OFF_ORIG:], pool=pool_w.astype(BF16), branch=w_branch.astype(BF16),
        o=w_o.astype(BF16), up=w_up.astype(BF16), down=w_down.astype(BF16))
    pos = jnp.arange(SEQ)
    cmp_end = jnp.asarray(np.arange(N_CMP_PAD) * CMP_STRIDE + CMP_BLOCK - 1)
    tabs = dict(tok=_nsa_rope_tables(pos) + _ret_rope_tables(pos), cmp=_nsa_rope_tables(cmp_end),
                ret=_retention_decay_tables(), ovt=_overlap_t(), onehot=_block_onehot())
    x2 = x.reshape(batch * seq, d)
    for l in range(depth):
        p = {k: v[l] for k, v in small.items()}
        x2 = _layer(x2, batch, l, p, wts, tabs, final_norm_g, final_norm=(l == depth - 1))
    return x2.reshape(batch, seq, d)
```

```python
import functools

import numpy as np
import jax
import jax.numpy as jnp
from jax import lax
from jax.experimental import pallas as pl
from jax.experimental.pallas import tpu as pltpu

F32 = jnp.float32
BF16 = jnp.bfloat16

D_MODEL = 1024
SEQ = 2048
BRANCH = 512
N_BRANCH = 4
CONV_TAPS = 3
NSA_HEADS = 8
NSA_GROUPS = 2
NSA_HPG = NSA_HEADS // NSA_GROUPS
NSA_DH = BRANCH // NSA_HEADS
CMP_BLOCK = 32
CMP_STRIDE = 16
N_CMP = (SEQ - CMP_BLOCK) // CMP_STRIDE + 1
N_CMP_PAD = SEQ // CMP_STRIDE
SEL_BLOCK = 64
N_SEL = SEQ // SEL_BLOCK
SEL_TOP_N = 16
WINDOW = 512
ROPE_THETA = 500000.0
ROPE_DIM = NSA_DH // 4
FORCE_SCORE = 1.0e4
RET_HEADS = 4
RET_DK = BRANCH // RET_HEADS
RET_CHUNK = 128
RET_THETA = 10000.0
POOL_WINDOWS = (2, 4, 8, 16)
POOL_GROUP_DIM = BRANCH // len(POOL_WINDOWS)
D_FF = 2816
EPS = 1e-6
NEG_INF = -1e30
LOG2_E = 1.4426950408889634
Q_SCALE = NSA_DH ** -0.5 * LOG2_E

SC_OFF = 0
Q_OFF = 3 * BRANCH
KV_OFF = Q_OFF + BRANCH
NG_OFF = KV_OFF + 3 * 2 * NSA_GROUPS * NSA_DH
NG_COLS = 3 * NSA_HEADS
NG_PAD = 128
RET_OFF = NG_OFF + NG_PAD
POOL_OFF = RET_OFF + 4 * BRANCH
MAIN_COLS = POOL_OFF + BRANCH
GATE_OFF_ORIG = NG_OFF + NG_COLS + 4 * BRANCH + BRANCH

IN_TILE = 1024
MERGE_TILE = 512
MLP_TILE = 1024
CONV_HALO = 8
POOL_HALO = 16
ATT_TQ = 256
ATT_TK = 256
KV_LANES = 4 * NSA_GROUPS * 128
CMP_TQ = 2048
VMEM_LIMIT = 56 * 1024 * 1024


def _const_spec(shape):
    n = len(shape)
    return pl.BlockSpec(shape, lambda *_: (0,) * n, pipeline_mode=pl.Buffered(1))


def _layer_spec(layer, shape):
    n = len(shape)
    return pl.BlockSpec((None,) + tuple(shape), lambda *_: (layer,) + (0,) * n, pipeline_mode=pl.Buffered(1))


def _rms(x, g):
    ms = jnp.mean(x * x, axis=-1, keepdims=True)
    return x * lax.rsqrt(ms + EPS) * g


def _dot(a, b):
    return jnp.dot(a, b, preferred_element_type=F32)


def _dot_nt(a, b):
    return lax.dot_general(a, b, (((1,), (1,)), ((), ())), preferred_element_type=F32)


def _inproj_kernel(x_ref, g_ref, wa_ref, wn_ref, wb_ref, scw_ref, nc_ref, nsa_ref, nsb_ref, rc_ref, rs_ref, oh_ref, pw_ref, ps_ref,
                   ya_ref, q_ref, kc_ref, vc_ref, kv2_ref, ng_ref, rq_ref, rg_ref, yd_ref, ch_buf, u_buf):
    T = IN_TILE
    tiles_per_seq = SEQ // T
    i = pl.program_id(0)

    @pl.when(i % tiles_per_seq == 0)
    def _():
        ch_buf[0:CONV_HALO, :] = jnp.zeros((CONV_HALO, BRANCH), F32)
        u_buf[0:POOL_HALO, :] = jnp.zeros((POOL_HALO, BRANCH), F32)

    h = _rms(x_ref[...], g_ref[...]).astype(BF16)

    def proj(off, width):
        if off < NG_OFF:
            return _dot(h, wa_ref[:, off:off + width])
        if off == NG_OFF:
            return _dot(h, wn_ref[...])
        return _dot(h, wb_ref[:, off - RET_OFF:off - RET_OFF + width])

    nc, nsa, nsb = nc_ref[...], nsa_ref[...], nsb_ref[...]
    rc, rs = rc_ref[...], rs_ref[...]
    W = 256

    def rope_nsa(x):
        return x * nc + pltpu.roll(x, 128 - ROPE_DIM // 2, 1) * nsa + pltpu.roll(x, ROPE_DIM // 2, 1) * nsb

    def rope_ret(x):
        return x * rc + pltpu.roll(x, RET_DK // 2, 1) * rs

    def short_conv(c0, zs):
        zb, zc, zh = zs
        cols = slice(c0, c0 + W)
        ch_buf[CONV_HALO:CONV_HALO + T, cols] = zc * zh
        conv = scw_ref[CONV_TAPS - 1:CONV_TAPS, cols] * ch_buf[CONV_HALO:CONV_HALO + T, cols]
        for k in range(1, CONV_TAPS):
            conv = conv + scw_ref[CONV_TAPS - 1 - k:CONV_TAPS - k, cols] * ch_buf[CONV_HALO - k:CONV_HALO - k + T, cols]
        ya_ref[:, cols] = (zb * conv).astype(BF16)
        ch_buf[0:CONV_HALO, cols] = ch_buf[T:T + CONV_HALO, cols]

    def pooling(c0, zs):
        (zu,) = zs
        u_buf[POOL_HALO:POOL_HALO + T, c0:c0 + W] = zu
        pos = (i % tiles_per_seq) * T + lax.broadcasted_iota(jnp.int32, (T, 1), 0)
        for gi in range(c0 // POOL_GROUP_DIM, (c0 + W) // POOL_GROUP_DIM):
            win = POOL_WINDOWS[gi]
            g0, g1 = gi * POOL_GROUP_DIM, (gi + 1) * POOL_GROUP_DIM
            s = u_buf[:, g0:g1]
            shift = 1
            while shift < win:
                s = s + pltpu.roll(s, shift, 0)
                shift *= 2
            cnt = jnp.minimum(pos + 1, win).astype(F32)
            pooled = s[POOL_HALO:, :] / cnt - zu[:, g0 - c0:g1 - c0]
            y = _dot(pooled.astype(BF16), pw_ref[gi]) * ps_ref[:, g0:g1]
            yd_ref[:, g0:g1] = y.astype(BF16)
        u_buf[0:POOL_HALO, c0:c0 + W] = u_buf[T:T + POOL_HALO, c0:c0 + W]

    def attn_q(c0, zs):
        for c in range(W // 128):
            lanes = slice(c0 + c * 128, c0 + (c + 1) * 128)
            q_ref[:, lanes] = (rope_nsa(zs[0][:, c * 128:(c + 1) * 128]) * Q_SCALE).astype(BF16)

    def cmp_kv(c0, zs):
        kc_ref[...] = zs[0][:, 0:128]
        vc_ref[...] = zs[0][:, 128:256]

    def attn_kv(c0, zs):
        branch = c0 // W - 1
        key_fill = oh_ref[...] if branch == 0 else jnp.zeros((T, NSA_DH), BF16)
        pieces = ((rope_nsa(zs[0][:, 0:128]).astype(BF16), key_fill),
                  (zs[0][:, 128:256].astype(BF16), jnp.ones((T, NSA_DH), BF16)))
        for c, (piece, fill) in enumerate(pieces):
            for g in range(NSA_GROUPS):
                lane0 = (2 * (2 * branch + c) + g) * 128
                kv2_ref[:, lane0:lane0 + 128] = jnp.concatenate([piece[:, g * NSA_DH:(g + 1) * NSA_DH], fill], axis=1)

    def attn_gates(c0, zs):
        ng_ref[...] = jax.nn.sigmoid(zs[0])

    def ret_q(c0, zs):
        for c in range(W // 128):
            rq_ref[:, c0 + c * 128:c0 + (c + 1) * 128] = rope_ret(zs[0][:, c * 128:(c + 1) * 128]).astype(BF16)

    def ret_k(c0, zs):
        for c in range(W // 128):
            rq_ref[:, BRANCH + c0 + c * 128:BRANCH + c0 + (c + 1) * 128] = (
                rope_ret(zs[0][:, c * 128:(c + 1) * 128]) * (RET_DK ** -0.5)).astype(BF16)

    def ret_v(c0, zs):
        rq_ref[:, 2 * BRANCH + c0:2 * BRANCH + c0 + W] = zs[0].astype(BF16)

    def ret_gate(c0, zs):
        rg_ref[:, c0:c0 + W] = zs[0]

    stages = []
    for c0 in range(0, BRANCH, W):
        stages.append(((SC_OFF + c0, SC_OFF + BRANCH + c0, SC_OFF + 2 * BRANCH + c0), W, c0, short_conv))
    for c0 in range(0, BRANCH, W):
        stages.append(((POOL_OFF + c0,), W, c0, pooling))
    for c0 in range(0, BRANCH, W):
        stages.append(((Q_OFF + c0,), W, c0, attn_q))
    stages.append(((KV_OFF,), W, 0, cmp_kv))
    stages.append(((KV_OFF + W,), W, W, attn_kv))
    stages.append(((KV_OFF + 2 * W,), W, 2 * W, attn_kv))
    stages.append(((NG_OFF,), NG_PAD, 0, attn_gates))
    for consumer, base in ((ret_q, RET_OFF), (ret_k, RET_OFF + BRANCH), (ret_v, RET_OFF + 2 * BRANCH),
                           (ret_gate, RET_OFF + 3 * BRANCH)):
        for c0 in range(0, BRANCH, W):
            stages.append(((base + c0,), W, c0, consumer))
    pending = None
    for offs, width, c0, consumer in stages:
        zs = tuple(proj(off, width) for off in offs)
        if pending is not None:
            pending[0](pending[1], pending[2])
        pending = (consumer, c0, zs)
    pending[0](pending[1], pending[2])


def _inproj(x2, g, w_a, w_ng, w_b, layer, scw, tabs, onehot, pool_w, pool_scale):
    n = x2.shape[0]
    T = IN_TILE
    tok = lambda width: pl.BlockSpec((T, width), lambda i: (i, 0))
    tab = pl.BlockSpec((T, 128), lambda i: (i % (SEQ // T), 0))
    out_shapes = (
        jax.ShapeDtypeStruct((n, BRANCH), BF16),
        jax.ShapeDtypeStruct((n, BRANCH), BF16),
        jax.ShapeDtypeStruct((n, 128), F32),
        jax.ShapeDtypeStruct((n, 128), F32),
        jax.ShapeDtypeStruct((n, KV_LANES), BF16),
        jax.ShapeDtypeStruct((n, NG_PAD), F32),
        jax.ShapeDtypeStruct((n, 3 * BRANCH), BF16),
        jax.ShapeDtypeStruct((n, BRANCH), F32),
        jax.ShapeDtypeStruct((n, BRANCH), BF16),
    )
    return pl.pallas_call(
        _inproj_kernel,
        name="inproj",
        grid=(n // T,),
        in_specs=[tok(D_MODEL), _const_spec((1, D_MODEL)), _layer_spec(layer, (D_MODEL, NG_OFF)),
                  _layer_spec(layer, (D_MODEL, NG_PAD)), _layer_spec(layer, (D_MODEL, MAIN_COLS - RET_OFF)),
                  _const_spec((CONV_TAPS, BRANCH)), tab, tab, tab, tab, tab,
                  pl.BlockSpec((T, NSA_DH), lambda i: (i % (SEQ // T), 0)),
                  _layer_spec(layer, (len(POOL_WINDOWS), POOL_GROUP_DIM, POOL_GROUP_DIM)), _const_spec((1, BRANCH))],
        out_specs=[tok(BRANCH), tok(BRANCH), tok(128), tok(128), tok(KV_LANES), tok(NG_PAD), tok(3 * BRANCH), tok(BRANCH),
                   tok(BRANCH)],
        out_shape=out_shapes,
        scratch_shapes=[pltpu.VMEM((CONV_HALO + T, BRANCH), F32), pltpu.VMEM((POOL_HALO + T, BRANCH), F32)],
        compiler_params=pltpu.CompilerParams(dimension_semantics=("arbitrary",), vmem_limit_bytes=VMEM_LIMIT),
    )(x2, g, w_a, w_ng, w_b, scw, *tabs, onehot, pool_w, pool_scale)


def _compress_kernel(kc_ref, vc_ref, pe_ref, w1_ref, w1f_ref, w2_ref, cc_ref, csa_ref, csb_ref, out_ref):
    for kv, x_ref in enumerate((kc_ref, vc_ref)):
        acc = jnp.zeros((N_CMP_PAD, 4 * NSA_DH), F32)
        for l in range(CMP_STRIDE):
            x = x_ref[0, pl.ds(l, N_CMP_PAD, stride=CMP_STRIDE), :].astype(BF16)
            acc = acc + _dot(x, w1_ref[kv, l])
        bias = _dot(pe_ref[...], w1f_ref[kv])[0:1, :]
        for g in range(NSA_GROUPS):
            a = acc[:, g * 128:(g + 1) * 128]
            hid = jax.nn.gelu(a + pltpu.roll(pltpu.roll(a, N_CMP_PAD - 1, 0), NSA_DH, 1) + bias)
            y = _dot(hid.astype(BF16), w2_ref[kv])
            if kv == 0:
                y = (y * cc_ref[...] + pltpu.roll(y, 128 - ROPE_DIM // 2, 1) * csa_ref[...]
                     + pltpu.roll(y, ROPE_DIM // 2, 1) * csb_ref[...])
            out_ref[0, kv * NSA_GROUPS + g] = y[:, 0:NSA_DH].astype(BF16)


def _compress(kc3, vc3, pe8, w1blk, w1f, w2p, ctabs):
    b = kc3.shape[0]
    return pl.pallas_call(
        _compress_kernel,
        name="compress",
        grid=(b,),
        in_specs=[pl.BlockSpec((1, SEQ, 128), lambda i: (i, 0, 0)), pl.BlockSpec((1, SEQ, 128), lambda i: (i, 0, 0)),
                  _const_spec((8, CMP_BLOCK * NSA_DH)), _const_spec((2, CMP_STRIDE, 128, 4 * NSA_DH)),
                  _const_spec((2, CMP_BLOCK * NSA_DH, 128)), _const_spec((2, 128, 128)),
                  _const_spec((N_CMP_PAD, 128)), _const_spec((N_CMP_PAD, 128)), _const_spec((N_CMP_PAD, 128))],
        out_specs=pl.BlockSpec((1, 4, N_CMP_PAD, NSA_DH), lambda i: (i, 0, 0, 0)),
        out_shape=jax.ShapeDtypeStruct((b, 4, N_CMP_PAD, NSA_DH), BF16),
        compiler_params=pltpu.CompilerParams(dimension_semantics=("arbitrary",), vmem_limit_bytes=VMEM_LIMIT),
    )(kc3, vc3, pe8, w1blk, w1f, w2p, *ctabs)


def _cmp_attn_kernel(q_ref, kc_ref, vc_ref, ovt_ref, o_ref, sb_ref):
    R = CMP_TQ
    r = pl.program_id(2)
    kc, vc = kc_ref[0, 0], vc_ref[0, 0]
    t_rows = r * R + lax.broadcasted_iota(jnp.int32, (R, N_CMP_PAD), 0)
    blk_end = lax.broadcasted_iota(jnp.int32, (R, N_CMP_PAD), 1) * CMP_STRIDE + (CMP_BLOCK - 1)
    valid = blk_end <= t_rows
    qs = jnp.concatenate([q_ref[0, :, j * NSA_DH:(j + 1) * NSA_DH] for j in range(NSA_HPG)], axis=0)
    sm = jnp.where(valid[None], _dot_nt(qs, kc).reshape(NSA_HPG, R, N_CMP_PAD), NEG_INF)
    e = jnp.exp2(sm - jnp.max(sm, axis=-1, keepdims=True))
    p = jnp.where(valid[None], e, 0.0) / jnp.sum(e, axis=-1, keepdims=True)
    o = _dot(p.reshape(NSA_HPG * R, N_CMP_PAD).astype(BF16), vc)
    for j in range(NSA_HPG):
        o_ref[0, :, j * NSA_DH:(j + 1) * NSA_DH] = o[j * R:(j + 1) * R]
    psum = jnp.sum(p, axis=0)
    imp = lax.dot_general(ovt_ref[...], psum, (((1,), (1,)), ((), ())), preferred_element_type=F32,
                          precision=lax.Precision.HIGHEST)
    cur = (r * R + lax.broadcasted_iota(jnp.int32, (N_SEL, R), 1)) // SEL_BLOCK
    bid = lax.broadcasted_iota(jnp.int32, (N_SEL, R), 0)
    forced = (bid == 0) | (bid == cur) | (bid == cur - 1)
    imp = jnp.where(forced, FORCE_SCORE, jnp.where(bid > cur, -FORCE_SCORE, imp))
    rank = jnp.zeros((N_SEL, R), F32)
    for m in range(N_SEL):
        a = imp[m:m + 1, :]
        before = jnp.where(bid > m, jnp.where(a >= imp, 1.0, 0.0), jnp.where(a > imp, 1.0, 0.0))
        rank = rank + before
    bias_t = jnp.where(rank < SEL_TOP_N, 0.0, NEG_INF)
    bias = bias_t.T
    sb_ref[0, 0] = jnp.concatenate([bias, jnp.zeros((R, NSA_DH - N_SEL), F32)], axis=1).astype(BF16)


def _cmp_attn(q3, kcvc, ovt):
    b = q3.shape[0]
    R = CMP_TQ
    gw = NSA_HPG * NSA_DH
    return pl.pallas_call(
        _cmp_attn_kernel,
        name="cmp_attn",
        grid=(b, NSA_GROUPS, SEQ // R),
        in_specs=[pl.BlockSpec((1, R, gw), lambda i, g, r: (i, r, g)),
                  pl.BlockSpec((1, 1, N_CMP_PAD, NSA_DH), lambda i, g, r: (i, g, 0, 0)),
                  pl.BlockSpec((1, 1, N_CMP_PAD, NSA_DH), lambda i, g, r: (i, NSA_GROUPS + g, 0, 0)),
                  _const_spec((N_SEL, N_CMP_PAD))],
        out_specs=[pl.BlockSpec((1, R, gw), lambda i, g, r: (i, r, g)),
                   pl.BlockSpec((1, 1, R, NSA_DH), lambda i, g, r: (i, g, r, 0))],
        out_shape=(jax.ShapeDtypeStruct((b, SEQ, BRANCH), F32),
                   jax.ShapeDtypeStruct((b, NSA_GROUPS, SEQ, NSA_DH), BF16)),
        compiler_params=pltpu.CompilerParams(dimension_semantics=("arbitrary",) * 3, vmem_limit_bytes=VMEM_LIMIT),
    )(q3, kcvc, kcvc, ovt)


def _sel_win_kernel(q_ref, kv_ref, sb_ref, acc_ref, qa_ref, m_ref):
    TQ, TK = ATT_TQ, ATT_TK
    M = NSA_HPG * TQ
    qi = pl.program_id(1)

    def chunk(rows, c, g):
        return kv_ref[0, rows, (2 * c + g) * 128:(2 * c + g + 1) * 128]

    for g in range(NSA_GROUPS):
        sb = sb_ref[0, g]
        for j in range(NSA_HPG):
            h = g * NSA_HPG + j
            qa_ref[g, j * TQ:(j + 1) * TQ, :] = jnp.concatenate([q_ref[0, :, h * NSA_DH:(h + 1) * NSA_DH], sb], axis=1)

    m_ref[...] = jnp.full(m_ref.shape, NEG_INF, F32)
    acc_ref[...] = jnp.zeros(acc_ref.shape, F32)

    def online(idx, s, v):
        m = m_ref[idx]
        m_new = jnp.maximum(m, jnp.max(s, axis=-1, keepdims=True))
        p = jnp.exp2(s - pltpu.repeat(m_new, s.shape[1] // 128, axis=1))
        acc_ref[0, 0, idx] = jnp.exp2(m - m_new) * acc_ref[0, 0, idx] + _dot(p.astype(BF16), v)
        m_ref[idx] = m_new

    def far_tiles(k0, width):
        rows = pl.ds(pl.multiple_of(k0, TK), width)
        for g in range(NSA_GROUPS):
            online(g, _dot_nt(qa_ref[g], chunk(rows, 0, g)), chunk(rows, 1, g))

    def near_tile(t):
        rows = pl.ds(pl.multiple_of(t * TK, TK), TK)
        dist = (lax.broadcasted_iota(jnp.int32, (TQ, TK), 0) - lax.broadcasted_iota(jnp.int32, (TQ, TK), 1)
                + (qi - t) * TK)
        causal_bias = jnp.where(dist >= 0, 0.0, NEG_INF)
        window_bias = jnp.where(lax.bitcast_convert_type(dist, jnp.uint32) < WINDOW, 0.0, NEG_INF)

        def masked(s, bias):
            return (s.reshape(NSA_HPG, TQ, TK) + bias[None]).reshape(M, TK)

        for g in range(NSA_GROUPS):
            qa = qa_ref[g]
            online(g, masked(_dot_nt(qa, chunk(rows, 0, g)), causal_bias), chunk(rows, 1, g))
            online(NSA_GROUPS + g, masked(_dot_nt(qa, chunk(rows, 2, g)), window_bias), chunk(rows, 3, g))

    n_far = jnp.maximum(qi - WINDOW // TK, 0)

    def far_pair(i, carry):
        far_tiles(i * (2 * TK), 2 * TK)
        return carry

    lax.fori_loop(0, n_far // 2, far_pair, 0)

    @pl.when(n_far % 2 == 1)
    def _():
        far_tiles((n_far - 1) * TK, TK)

    max_near = WINDOW // TK + 1
    for count in range(1, max_near + 1):
        @pl.when((qi == count - 1) if count < max_near else (qi >= count - 1))
        def _(count=count):
            for back in range(count - 1, -1, -1):
                near_tile(qi - back)


def _sel_win(q3, kv3, selbias):
    b = q3.shape[0]
    TQ = ATT_TQ
    M = NSA_HPG * TQ
    return pl.pallas_call(
        _sel_win_kernel,
        name="sel_win_attn",
        grid=(b, SEQ // TQ),
        in_specs=[pl.BlockSpec((1, TQ, BRANCH), lambda i, t: (i, t, 0)),
                  pl.BlockSpec((1, SEQ, KV_LANES), lambda i, t: (i, 0, 0)),
                  pl.BlockSpec((1, NSA_GROUPS, TQ, NSA_DH), lambda i, t: (i, 0, t, 0))],
        out_specs=pl.BlockSpec((1, 1, 2 * NSA_GROUPS, M, 128), lambda i, t: (i, t, 0, 0, 0)),
        out_shape=jax.ShapeDtypeStruct((b, SEQ // TQ, 2 * NSA_GROUPS, M, 128), F32),
        scratch_shapes=[pltpu.VMEM((NSA_GROUPS, M, 128), BF16), pltpu.VMEM((2 * NSA_GROUPS, M, 128), F32)],
        compiler_params=pltpu.CompilerParams(dimension_semantics=("arbitrary", "arbitrary"),
                                             vmem_limit_bytes=VMEM_LIMIT),
    )(q3, kv3, selbias)


def _retention_kernel(qkv_ref, zg_ref, intra_ref, qd_ref, kd_ref, cd_ref, gn_ref, y_ref, st_ref):
    C = RET_CHUNK
    NC = SEQ // C

    def bdot(a, b, ca, cb):
        return lax.dot_general(a, b, (((ca,), (cb,)), ((0,), (0,))), preferred_element_type=F32)

    for h in range(RET_HEADS):
        lanes = slice(h * RET_DK, (h + 1) * RET_DK)
        q3, k3, v3 = (qkv_ref[0, :, part * BRANCH + h * RET_DK:part * BRANCH + (h + 1) * RET_DK].reshape(NC, C, RET_DK)
                      for part in range(3))
        scores = bdot(q3, k3, 2, 2) * intra_ref[h][None]
        inner = bdot(scores.astype(BF16), v3, 2, 1)
        kdec = (k3.astype(F32) * kd_ref[h][None]).astype(BF16)
        kv = bdot(kdec, v3, 1, 1)
        cd = cd_ref[h, 0:1, :]
        state = jnp.zeros((RET_DK, RET_DK), F32)
        for c in range(NC):
            st_ref[c] = state.astype(BF16)
            state = state * cd + kv[c]
        qdec = (q3.astype(F32) * qd_ref[h][None]).astype(BF16)
        o = inner + bdot(qdec, st_ref[...], 2, 1)
        mu = jnp.mean(o, axis=-1, keepdims=True)
        d = o - mu
        var = jnp.mean(d * d, axis=-1, keepdims=True)
        o = (d * lax.rsqrt(var + EPS)).reshape(SEQ, RET_DK) * gn_ref[:, lanes]
        y_ref[0, :, lanes] = (o * jax.nn.silu(zg_ref[0, :, lanes])).astype(BF16)


def _retention(rq3, rg3, intra, qd, kd, cd, gn):
    b = rq3.shape[0]
    return pl.pallas_call(
        _retention_kernel,
        name="retention",
        grid=(b,),
        in_specs=[pl.BlockSpec((1, SEQ, 3 * BRANCH), lambda i: (i, 0, 0)),
                  pl.BlockSpec((1, SEQ, BRANCH), lambda i: (i, 0, 0)),
                  _const_spec((RET_HEADS, RET_CHUNK, RET_DK)), _const_spec((RET_HEADS, RET_CHUNK, RET_DK)),
                  _const_spec((RET_HEADS, RET_CHUNK, RET_DK)), _const_spec((RET_HEADS, 8, RET_DK)),
                  _const_spec((1, BRANCH))],
        out_specs=pl.BlockSpec((1, SEQ, BRANCH), lambda i: (i, 0, 0)),
        out_shape=jax.ShapeDtypeStruct((b, SEQ, BRANCH), BF16),
        scratch_shapes=[pltpu.VMEM((SEQ // RET_CHUNK, RET_DK, RET_DK), BF16)],
        compiler_params=pltpu.CompilerParams(dimension_semantics=("arbitrary",), vmem_limit_bytes=VMEM_LIMIT),
    )(rq3, rg3, intra, qd, kd, cd, gn)


def _merge_kernel(x_ref, g_ref, wg_ref, bg_ref, ya_ref, oc_ref, acc_ref, ng_ref, yc_ref, yd_ref, wb_ref, wo_ref, o_ref):
    T = MERGE_TILE
    TQ = ATT_TQ
    x = x_ref[...]
    h = _rms(x, g_ref[...]).astype(BF16)

    lane = lax.broadcasted_iota(jnp.int32, (TQ, 128), 1)

    def branch_out(br):
        rows = []
        for part in range(T // TQ):
            pairs = []
            for g in range(NSA_GROUPS):
                halves = []
                for j in range(NSA_HPG):
                    acc = acc_ref[part, 0, br * NSA_GROUPS + g, j * TQ:(j + 1) * TQ, :]
                    rolled = pltpu.roll(acc, NSA_DH, 1)
                    halves.append(acc * (1.0 / rolled) if j % 2 == 0 else rolled * (1.0 / acc))
                pairs += [jnp.where(lane < NSA_DH, halves[2 * c], halves[2 * c + 1]) for c in range(NSA_HPG // 2)]
            rows.append(jnp.concatenate(pairs, axis=1))
        return jnp.concatenate(rows, axis=0)

    def head_gates(br):
        return jnp.concatenate([jnp.broadcast_to(ng[:, 3 * hd + br:3 * hd + br + 1], (T, NSA_DH))
                                for hd in range(NSA_HEADS)], axis=1)

    W = 256

    def stage_dots(n, y, c):
        cols = slice(n * D_MODEL + c, n * D_MODEL + c + W)
        return _dot(h, wg_ref[:, cols]), _dot(y, wb_ref[n, :, c:c + W]), bg_ref[:, cols]

    merged = [None] * (D_MODEL // W)

    def stage_gate(c, dots):
        logits, proj, bias = dots
        term = proj * jax.nn.sigmoid(logits + bias)
        merged[c // W] = term if merged[c // W] is None else merged[c // W] + term

    pending = None

    def run_branch(n, y):
        nonlocal pending
        for c in range(0, D_MODEL, W):
            dots = stage_dots(n, y, c)
            if pending is not None:
                stage_gate(*pending)
            pending = (c, dots)

    ng = ng_ref[...]
    run_branch(0, ya_ref[...])
    yb = head_gates(0) * oc_ref[...] + head_gates(1) * branch_out(0)
    run_branch(2, yc_ref[...])
    yb = yb + head_gates(2) * branch_out(1)
    run_branch(3, yd_ref[...])
    run_branch(1, yb.astype(BF16))
    stage_gate(*pending)
    o_ref[...] = x + _dot(jnp.concatenate(merged, axis=1).astype(BF16), wo_ref[...])


def _merge(x2, g, wg, layer, bg, ya, oc, acc, ng, yc, yd, wb, wo):
    n = x2.shape[0]
    T = MERGE_TILE
    tok = lambda width: pl.BlockSpec((T, width), lambda i: (i, 0))
    return pl.pallas_call(
        _merge_kernel,
        name="merge",
        grid=(n // T,),
        in_specs=[tok(D_MODEL), _const_spec((1, D_MODEL)), _layer_spec(layer, (D_MODEL, N_BRANCH * D_MODEL)),
                  _const_spec((1, N_BRANCH * D_MODEL)), tok(BRANCH), tok(BRANCH),
                  pl.BlockSpec((T // ATT_TQ, 1) + acc.shape[2:], lambda i: (i, 0, 0, 0, 0)), tok(NG_PAD),
                  tok(BRANCH), tok(BRANCH),
                  _layer_spec(layer, (N_BRANCH, BRANCH, D_MODEL)), _layer_spec(layer, (D_MODEL, D_MODEL))],
        out_specs=tok(D_MODEL),
        out_shape=jax.ShapeDtypeStruct((n, D_MODEL), F32),
        compiler_params=pltpu.CompilerParams(dimension_semantics=("arbitrary",), vmem_limit_bytes=VMEM_LIMIT),
    )(x2, g, wg, bg, ya, oc, acc, ng, yc, yd, wb, wo)


FF_CHUNK = 256


def _mlp_kernel(x_ref, g_ref, wup_ref, cw_ref, wdn_ref, fg_ref, o_ref, ug_buf, act_buf, *, final_norm):
    T = MLP_TILE
    i = pl.program_id(0)

    @pl.when(i % (SEQ // T) == 0)
    def _():
        ug_buf[0:CONV_HALO, :] = jnp.zeros((CONV_HALO, D_FF), F32)

    x = x_ref[...]
    h = _rms(x, g_ref[...]).astype(BF16)
    for c in range(D_FF // FF_CHUNK):
        cols = slice(c * FF_CHUNK, (c + 1) * FF_CHUNK)
        ug_buf[CONV_HALO:CONV_HALO + T, cols] = _dot(h, wup_ref[:, cols])
        conv = cw_ref[CONV_TAPS - 1:CONV_TAPS, cols] * ug_buf[CONV_HALO:CONV_HALO + T, cols]
        for k in range(1, CONV_TAPS):
            conv = conv + cw_ref[CONV_TAPS - 1 - k:CONV_TAPS - k, cols] * ug_buf[CONV_HALO - k:CONV_HALO - k + T, cols]
        val = _dot(h, wup_ref[:, D_FF + c * FF_CHUNK:D_FF + (c + 1) * FF_CHUNK])
        act_buf[:, cols] = (jax.nn.silu(conv) * val).astype(BF16)
    ug_buf[0:CONV_HALO, :] = ug_buf[T:T + CONV_HALO, :]
    y = x + _dot(act_buf[...], wdn_ref[...])
    if final_norm:
        y = _rms(y, fg_ref[...])
    o_ref[...] = y


def _mlp(x2, g, wup, layer, cw, wdn, fg, final_norm):
    n = x2.shape[0]
    T = MLP_TILE
    tok = pl.BlockSpec((T, D_MODEL), lambda i: (i, 0))
    return pl.pallas_call(
        functools.partial(_mlp_kernel, final_norm=final_norm),
        name="mlp",
        grid=(n // T,),
        in_specs=[tok, _const_spec((1, D_MODEL)), _layer_spec(layer, (D_MODEL, 2 * D_FF)), _const_spec((CONV_TAPS, D_FF)),
                  _layer_spec(layer, (D_FF, D_MODEL)), _const_spec((1, D_MODEL))],
        out_specs=tok,
        out_shape=jax.ShapeDtypeStruct((n, D_MODEL), F32),
        scratch_shapes=[pltpu.VMEM((CONV_HALO + T, D_FF), F32), pltpu.VMEM((T, D_FF), BF16)],
        compiler_params=pltpu.CompilerParams(dimension_semantics=("arbitrary",), vmem_limit_bytes=VMEM_LIMIT),
    )(x2, g, wup, cw, wdn, fg)


def _rope_angles(pos, half, theta):
    inv_freq = np.power(np.float32(theta), -np.arange(half, dtype=np.float32) / np.float32(half))
    ang = pos.astype(F32)[:, None] * jnp.asarray(inv_freq)[None, :]
    return jnp.cos(ang), jnp.sin(ang)


def _nsa_rope_tables(pos):
    half = ROPE_DIM // 2
    cos, sin = _rope_angles(pos, half, ROPE_THETA)
    n = pos.shape[0]
    ones = jnp.ones((n, NSA_DH - ROPE_DIM), F32)
    zeros = jnp.zeros((n, NSA_DH - ROPE_DIM), F32)
    zh = jnp.zeros((n, half), F32)
    c = jnp.concatenate([cos, cos, ones], axis=1)
    sa = jnp.concatenate([-sin, zh, zeros], axis=1)
    sb = jnp.concatenate([zh, sin, zeros], axis=1)
    return tuple(jnp.tile(t, (1, 128 // NSA_DH)) for t in (c, sa, sb))


def _ret_rope_tables(pos):
    half = RET_DK // 2
    cos, sin = _rope_angles(pos, half, RET_THETA)
    return jnp.concatenate([cos, cos], axis=1), jnp.concatenate([-sin, sin], axis=1)


def _retention_decay_tables():
    H, C = RET_HEADS, RET_CHUNK
    log_g = np.log1p(-np.exp2(-5.0 - np.arange(H))).astype(np.float32)
    n = np.arange(C, dtype=np.float32)
    diff = n[:, None] - n[None, :]
    intra = np.where(diff >= 0, np.exp(np.maximum(diff, 0.0)[None] * log_g[:, None, None]), 0.0).astype(np.float32)
    q_dec = np.exp((n[None, :] + 1.0) * log_g[:, None]).astype(np.float32)
    k_dec = np.exp((C - 1.0 - n[None, :]) * log_g[:, None]).astype(np.float32)
    c_dec = np.exp(C * log_g).astype(np.float32)
    qd = np.broadcast_to(q_dec[:, :, None], (H, C, RET_DK))
    kd = np.broadcast_to(k_dec[:, :, None], (H, C, RET_DK))
    cd = np.broadcast_to(c_dec[:, None, None], (H, 8, RET_DK))
    return tuple(jnp.asarray(np.ascontiguousarray(t)) for t in (intra, qd, kd, cd))


def _overlap_t():
    cs0 = np.arange(N_CMP_PAD) * CMP_STRIDE
    ss0 = np.arange(N_SEL) * SEL_BLOCK
    ov = np.clip(np.minimum(cs0[:, None] + CMP_BLOCK, ss0[None, :] + SEL_BLOCK)
                 - np.maximum(cs0[:, None], ss0[None, :]), 0, None)
    ov = (ov / CMP_BLOCK).astype(np.float32)
    ov[N_CMP:, :] = 0.0
    return jnp.asarray(np.ascontiguousarray(ov.T))


def _block_onehot():
    e = np.zeros((SEQ, NSA_DH), np.float32)
    e[np.arange(SEQ), np.arange(SEQ) // SEL_BLOCK] = 1.0
    return jnp.asarray(e, dtype=BF16)


def _layer(x2, batch, layer, p, wts, tabs, final_g, final_norm):
    n = x2.shape[0]
    ya, q, kc_in, vc_in, kv2, ng, rq, rg, yd = _inproj(
        x2, p["norm1_g"][None, :], wts["main_a"], wts["main_ng"], wts["main_b"], layer, p["sc_conv"], tabs["tok"], tabs["onehot"], wts["pool"],
        p["pool_scale"][None, :])

    w1 = jnp.stack([p["cmp_w1_k"], p["cmp_w1_v"]])
    wcat = jnp.concatenate([w1[:, :CMP_STRIDE], w1[:, CMP_STRIDE:]], axis=3)
    zero = jnp.zeros_like(wcat)
    w1blk = jnp.concatenate([jnp.concatenate([wcat, zero], axis=3), jnp.concatenate([zero, wcat], axis=3)],
                            axis=2).astype(BF16)
    w1f = w1.reshape(2, CMP_BLOCK * NSA_DH, NSA_DH)
    w1f = jnp.concatenate([w1f, jnp.zeros_like(w1f)], axis=2).astype(BF16)
    pe8 = jnp.broadcast_to(p["cmp_pe"].reshape(1, -1), (8, CMP_BLOCK * NSA_DH)).astype(BF16)
    w2 = jnp.stack([p["cmp_w2_k"], p["cmp_w2_v"]])
    w2p = jnp.pad(w2, ((0, 0), (0, 128 - NSA_DH), (0, 128 - NSA_DH))).astype(BF16)
    kcvc = _compress(kc_in.reshape(batch, SEQ, 128), vc_in.reshape(batch, SEQ, 128), pe8, w1blk, w1f, w2p,
                     tabs["cmp"])

    q3 = q.reshape(batch, SEQ, BRANCH)
    ocmp, selbias = _cmp_attn(q3, kcvc, tabs["ovt"])
    acc = _sel_win(q3, kv2.reshape(batch, SEQ, KV_LANES), selbias)

    yc = _retention(rq.reshape(batch, SEQ, 3 * BRANCH), rg.reshape(batch, SEQ, BRANCH), *tabs["ret"],
                    p["ret_gn_g"][None, :])

    x2 = _merge(x2, p["norm1_g"][None, :], wts["gate"], layer, p["b_gate"][None, :], ya, ocmp.reshape(n, BRANCH),
                acc.reshape((n // ATT_TQ, 1) + acc.shape[2:]), ng, yc.reshape(n, BRANCH), yd,
                wts["branch"], wts["o"])
    return _mlp(x2, p["norm2_g"][None, :], wts["up"], layer, p["ffn_conv"], wts["down"], final_g[None, :], final_norm)


def kernel(x, norm1_g, w_in, b_gate, sc_conv, cmp_pe, cmp_w1_k, cmp_w2_k, cmp_w1_v, cmp_w2_v, ret_gn_g, pool_w, pool_scale, w_branch, w_o, norm2_g, w_up, ffn_conv, w_down, final_norm_g):
    batch, seq, d = x.shape
    assert seq == SEQ and d == D_MODEL
    depth = w_in.shape[0]
    small = dict(norm1_g=norm1_g, b_gate=b_gate, sc_conv=sc_conv, cmp_pe=cmp_pe, cmp_w1_k=cmp_w1_k,
                 cmp_w2_k=cmp_w2_k, cmp_w1_v=cmp_w1_v, cmp_w2_v=cmp_w2_v, ret_gn_g=ret_gn_g, pool_scale=pool_scale,
                 norm2_g=norm2_g, ffn_conv=ffn_conv)
    n_main = NG_OFF + NG_COLS
    w_in16 = w_in.astype(BF16)
    wts = dict(
        main_a=w_in16[:, :, :NG_OFF],
        main_ng=jnp.pad(w_in16[:, :, NG_OFF:n_main], ((0, 0), (0, 0), (0, NG_PAD - NG_COLS))),
        main_b=w_in16[:, :, n_main:GATE_OFF_ORIG],
        gate=w_in16[:, :, GATE_OFF_ORIG:], pool=pool_w.astype(BF16), branch=w_branch.astype(BF16),
        o=w_o.astype(BF16), up=w_up.astype(BF16), down=w_down.astype(BF16))
    pos = jnp.arange(SEQ)
    cmp_end = jnp.asarray(np.arange(N_CMP_PAD) * CMP_STRIDE + CMP_BLOCK - 1)
    tabs = dict(tok=_nsa_rope_tables(pos) + _ret_rope_tables(pos), cmp=_nsa_rope_tables(cmp_end),
                ret=_retention_decay_tables(), ovt=_overlap_t(), onehot=_block_onehot())
    x2 = x.reshape(batch * seq, d)
    for l in range(depth):
        p = {k: v[l] for k, v in small.items()}
        x2 = _layer(x2, batch, l, p, wts, tabs, final_norm_g, final_norm=(l == depth - 1))
    return x2.reshape(batch, seq, d)
```

```python
import functools

import numpy as np
import jax
import jax.numpy as jnp
from jax import lax
from jax.experimental import pallas as pl
from jax.experimental.pallas import tpu as pltpu

F32 = jnp.float32
BF16 = jnp.bfloat16

D_MODEL = 1024
SEQ = 2048
BRANCH = 512
N_BRANCH = 4
CONV_TAPS = 3
NSA_HEADS = 8
NSA_GROUPS = 2
NSA_HPG = NSA_HEADS // NSA_GROUPS
NSA_DH = BRANCH // NSA_HEADS
CMP_BLOCK = 32
CMP_STRIDE = 16
N_CMP = (SEQ - CMP_BLOCK) // CMP_STRIDE + 1
N_CMP_PAD = SEQ // CMP_STRIDE
SEL_BLOCK = 64
N_SEL = SEQ // SEL_BLOCK
SEL_TOP_N = 16
WINDOW = 512
ROPE_THETA = 500000.0
ROPE_DIM = NSA_DH // 4
FORCE_SCORE = 1.0e4
RET_HEADS = 4
RET_DK = BRANCH // RET_HEADS
RET_CHUNK = 128
RET_THETA = 10000.0
POOL_WINDOWS = (2, 4, 8, 16)
POOL_GROUP_DIM = BRANCH // len(POOL_WINDOWS)
D_FF = 2816
EPS = 1e-6
NEG_INF = -1e30
LOG2_E = 1.4426950408889634
Q_SCALE = NSA_DH ** -0.5 * LOG2_E

SC_OFF = 0
Q_OFF = 3 * BRANCH
KV_OFF = Q_OFF + BRANCH
NG_OFF = KV_OFF + 3 * 2 * NSA_GROUPS * NSA_DH
NG_COLS = 3 * NSA_HEADS
NG_PAD = 128
RET_OFF = NG_OFF + NG_PAD
POOL_OFF = RET_OFF + 4 * BRANCH
MAIN_COLS = POOL_OFF + BRANCH
GATE_OFF_ORIG = NG_OFF + NG_COLS + 4 * BRANCH + BRANCH

IN_TILE = 1024
MERGE_TILE = 512
MLP_TILE = 1024
CONV_HALO = 8
POOL_HALO = 16
ATT_TQ = 256
ATT_TK = 256
KV_LANES = 4 * NSA_GROUPS * 128
CMP_TQ = 2048
VMEM_LIMIT = 56 * 1024 * 1024


def _const_spec(shape):
    n = len(shape)
    return pl.BlockSpec(shape, lambda *_: (0,) * n, pipeline_mode=pl.Buffered(1))


def _layer_spec(layer, shape):
    n = len(shape)
    return pl.BlockSpec((None,) + tuple(shape), lambda *_: (layer,) + (0,) * n, pipeline_mode=pl.Buffered(1))


def _rms(x, g):
    ms = jnp.mean(x * x, axis=-1, keepdims=True)
    return x * lax.rsqrt(ms + EPS) * g


def _dot(a, b):
    return jnp.dot(a, b, preferred_element_type=F32)


def _dot_nt(a, b):
    return lax.dot_general(a, b, (((1,), (1,)), ((), ())), preferred_element_type=F32)


def _inproj_kernel(x_ref, g_ref, wa_ref, wn_ref, wb_ref, scw_ref, nc_ref, nsa_ref, nsb_ref, rc_ref, rs_ref, oh_ref, pw_ref, ps_ref,
                   ya_ref, q_ref, kc_ref, vc_ref, kv2_ref, ng_ref, rq_ref, rg_ref, yd_ref, ch_buf, u_buf):
    T = IN_TILE
    tiles_per_seq = SEQ // T
    i = pl.program_id(0)

    @pl.when(i % tiles_per_seq == 0)
    def _():
        ch_buf[0:CONV_HALO, :] = jnp.zeros((CONV_HALO, BRANCH), F32)
        u_buf[0:POOL_HALO, :] = jnp.zeros((POOL_HALO, BRANCH), F32)

    h = _rms(x_ref[...], g_ref[...]).astype(BF16)

    def proj(off, width):
        if off < NG_OFF:
            return _dot(h, wa_ref[:, off:off + width])
        if off == NG_OFF:
            return _dot(h, wn_ref[...])
        return _dot(h, wb_ref[:, off - RET_OFF:off - RET_OFF + width])

    nc, nsa, nsb = nc_ref[...], nsa_ref[...], nsb_ref[...]
    rc, rs = rc_ref[...], rs_ref[...]
    W = 256
    STAGE_LOOKAHEAD = 2

    def rope_nsa(x):
        return x * nc + pltpu.roll(x, 128 - ROPE_DIM // 2, 1) * nsa + pltpu.roll(x, ROPE_DIM // 2, 1) * nsb

    def rope_ret(x):
        return x * rc + pltpu.roll(x, RET_DK // 2, 1) * rs

    def short_conv(c0, zs):
        zb, zc, zh = zs
        cols = slice(c0, c0 + W)
        ch_buf[CONV_HALO:CONV_HALO + T, cols] = zc * zh
        conv = scw_ref[CONV_TAPS - 1:CONV_TAPS, cols] * ch_buf[CONV_HALO:CONV_HALO + T, cols]
        for k in range(1, CONV_TAPS):
            conv = conv + scw_ref[CONV_TAPS - 1 - k:CONV_TAPS - k, cols] * ch_buf[CONV_HALO - k:CONV_HALO - k + T, cols]
        ya_ref[:, cols] = (zb * conv).astype(BF16)
        ch_buf[0:CONV_HALO, cols] = ch_buf[T:T + CONV_HALO, cols]

    def pooling(c0, zs):
        (zu,) = zs
        u_buf[POOL_HALO:POOL_HALO + T, c0:c0 + W] = zu
        pos = (i % tiles_per_seq) * T + lax.broadcasted_iota(jnp.int32, (T, 1), 0)
        for gi in range(c0 // POOL_GROUP_DIM, (c0 + W) // POOL_GROUP_DIM):
            win = POOL_WINDOWS[gi]
            g0, g1 = gi * POOL_GROUP_DIM, (gi + 1) * POOL_GROUP_DIM
            s = u_buf[:, g0:g1]
            shift = 1
            while shift < win:
                s = s + pltpu.roll(s, shift, 0)
                shift *= 2
            cnt = jnp.minimum(pos + 1, win).astype(F32)
            pooled = s[POOL_HALO:, :] / cnt - zu[:, g0 - c0:g1 - c0]
            y = _dot(pooled.astype(BF16), pw_ref[gi]) * ps_ref[:, g0:g1]
            yd_ref[:, g0:g1] = y.astype(BF16)
        u_buf[0:POOL_HALO, c0:c0 + W] = u_buf[T:T + POOL_HALO, c0:c0 + W]

    def attn_q(c0, zs):
        for c in range(W // 128):
            lanes = slice(c0 + c * 128, c0 + (c + 1) * 128)
            q_ref[:, lanes] = (rope_nsa(zs[0][:, c * 128:(c + 1) * 128]) * Q_SCALE).astype(BF16)

    def cmp_kv(c0, zs):
        kc_ref[...] = zs[0][:, 0:128]
        vc_ref[...] = zs[0][:, 128:256]

    def attn_kv(c0, zs):
        branch = c0 // W - 1
        key_fill = oh_ref[...] if branch == 0 else jnp.zeros((T, NSA_DH), BF16)
        pieces = ((rope_nsa(zs[0][:, 0:128]).astype(BF16), key_fill),
                  (zs[0][:, 128:256].astype(BF16), jnp.ones((T, NSA_DH), BF16)))
        for c, (piece, fill) in enumerate(pieces):
            for g in range(NSA_GROUPS):
                lane0 = (2 * (2 * branch + c) + g) * 128
                kv2_ref[:, lane0:lane0 + 128] = jnp.concatenate([piece[:, g * NSA_DH:(g + 1) * NSA_DH], fill], axis=1)

    def attn_gates(c0, zs):
        ng_ref[...] = jax.nn.sigmoid(zs[0])

    def ret_q(c0, zs):
        for c in range(W // 128):
            rq_ref[:, c0 + c * 128:c0 + (c + 1) * 128] = rope_ret(zs[0][:, c * 128:(c + 1) * 128]).astype(BF16)

    def ret_k(c0, zs):
        for c in range(W // 128):
            rq_ref[:, BRANCH + c0 + c * 128:BRANCH + c0 + (c + 1) * 128] = (
                rope_ret(zs[0][:, c * 128:(c + 1) * 128]) * (RET_DK ** -0.5)).astype(BF16)

    def ret_v(c0, zs):
        rq_ref[:, 2 * BRANCH + c0:2 * BRANCH + c0 + W] = zs[0].astype(BF16)

    def ret_gate(c0, zs):
        rg_ref[:, c0:c0 + W] = zs[0]

    stages = []
    for c0 in range(0, BRANCH, W):
        stages.append(((SC_OFF + c0, SC_OFF + BRANCH + c0, SC_OFF + 2 * BRANCH + c0), W, c0, short_conv))
    for c0 in range(0, BRANCH, W):
        stages.append(((POOL_OFF + c0,), W, c0, pooling))
    for c0 in range(0, BRANCH, W):
        stages.append(((Q_OFF + c0,), W, c0, attn_q))
    stages.append(((KV_OFF,), W, 0, cmp_kv))
    stages.append(((KV_OFF + W,), W, W, attn_kv))
    stages.append(((KV_OFF + 2 * W,), W, 2 * W, attn_kv))
    stages.append(((NG_OFF,), NG_PAD, 0, attn_gates))
    for consumer, base in ((ret_q, RET_OFF), (ret_k, RET_OFF + BRANCH), (ret_v, RET_OFF + 2 * BRANCH),
                           (ret_gate, RET_OFF + 3 * BRANCH)):
        for c0 in range(0, BRANCH, W):
            stages.append(((base + c0,), W, c0, consumer))
    queue = []
    for offs, width, c0, consumer in stages:
        queue.append((consumer, c0, tuple(proj(off, width) for off in offs)))
        if len(queue) > STAGE_LOOKAHEAD:
            ready, ready_c0, ready_zs = queue.pop(0)
            ready(ready_c0, ready_zs)
    for ready, ready_c0, ready_zs in queue:
        ready(ready_c0, ready_zs)


def _inproj(x2, g, w_a, w_ng, w_b, layer, scw, tabs, onehot, pool_w, pool_scale):
    n = x2.shape[0]
    T = IN_TILE
    tok = lambda width: pl.BlockSpec((T, width), lambda i: (i, 0))
    tab = pl.BlockSpec((T, 128), lambda i: (i % (SEQ // T), 0))
    out_shapes = (
        jax.ShapeDtypeStruct((n, BRANCH), BF16),
        jax.ShapeDtypeStruct((n, BRANCH), BF16),
        jax.ShapeDtypeStruct((n, 128), F32),
        jax.ShapeDtypeStruct((n, 128), F32),
        jax.ShapeDtypeStruct((n, KV_LANES), BF16),
        jax.ShapeDtypeStruct((n, NG_PAD), F32),
        jax.ShapeDtypeStruct((n, 3 * BRANCH), BF16),
        jax.ShapeDtypeStruct((n, BRANCH), F32),
        jax.ShapeDtypeStruct((n, BRANCH), BF16),
    )
    return pl.pallas_call(
        _inproj_kernel,
        name="inproj",
        grid=(n // T,),
        in_specs=[tok(D_MODEL), _const_spec((1, D_MODEL)), _layer_spec(layer, (D_MODEL, NG_OFF)),
                  _layer_spec(layer, (D_MODEL, NG_PAD)), _layer_spec(layer, (D_MODEL, MAIN_COLS - RET_OFF)),
                  _const_spec((CONV_TAPS, BRANCH)), tab, tab, tab, tab, tab,
                  pl.BlockSpec((T, NSA_DH), lambda i: (i % (SEQ // T), 0)),
                  _layer_spec(layer, (len(POOL_WINDOWS), POOL_GROUP_DIM, POOL_GROUP_DIM)), _const_spec((1, BRANCH))],
        out_specs=[tok(BRANCH), tok(BRANCH), tok(128), tok(128), tok(KV_LANES), tok(NG_PAD), tok(3 * BRANCH), tok(BRANCH),
                   tok(BRANCH)],
        out_shape=out_shapes,
        scratch_shapes=[pltpu.VMEM((CONV_HALO + T, BRANCH), F32), pltpu.VMEM((POOL_HALO + T, BRANCH), F32)],
        compiler_params=pltpu.CompilerParams(dimension_semantics=("arbitrary",), vmem_limit_bytes=VMEM_LIMIT),
    )(x2, g, w_a, w_ng, w_b, scw, *tabs, onehot, pool_w, pool_scale)


def _compress_kernel(kc_ref, vc_ref, pe_ref, w1_ref, w1f_ref, w2_ref, cc_ref, csa_ref, csb_ref, out_ref):
    for kv, x_ref in enumerate((kc_ref, vc_ref)):
        acc = jnp.zeros((N_CMP_PAD, 4 * NSA_DH), F32)
        for l in range(CMP_STRIDE):
            x = x_ref[0, pl.ds(l, N_CMP_PAD, stride=CMP_STRIDE), :].astype(BF16)
            acc = acc + _dot(x, w1_ref[kv, l])
        bias = _dot(pe_ref[...], w1f_ref[kv])[0:1, :]
        for g in range(NSA_GROUPS):
            a = acc[:, g * 128:(g + 1) * 128]
            hid = jax.nn.gelu(a + pltpu.roll(pltpu.roll(a, N_CMP_PAD - 1, 0), NSA_DH, 1) + bias)
            y = _dot(hid.astype(BF16), w2_ref[kv])
            if kv == 0:
                y = (y * cc_ref[...] + pltpu.roll(y, 128 - ROPE_DIM // 2, 1) * csa_ref[...]
                     + pltpu.roll(y, ROPE_DIM // 2, 1) * csb_ref[...])
            out_ref[0, kv * NSA_GROUPS + g] = y[:, 0:NSA_DH].astype(BF16)


def _compress(kc3, vc3, pe8, w1blk, w1f, w2p, ctabs):
    b = kc3.shape[0]
    return pl.pallas_call(
        _compress_kernel,
        name="compress",
        grid=(b,),
        in_specs=[pl.BlockSpec((1, SEQ, 128), lambda i: (i, 0, 0)), pl.BlockSpec((1, SEQ, 128), lambda i: (i, 0, 0)),
                  _const_spec((8, CMP_BLOCK * NSA_DH)), _const_spec((2, CMP_STRIDE, 128, 4 * NSA_DH)),
                  _const_spec((2, CMP_BLOCK * NSA_DH, 128)), _const_spec((2, 128, 128)),
                  _const_spec((N_CMP_PAD, 128)), _const_spec((N_CMP_PAD, 128)), _const_spec((N_CMP_PAD, 128))],
        out_specs=pl.BlockSpec((1, 4, N_CMP_PAD, NSA_DH), lambda i: (i, 0, 0, 0)),
        out_shape=jax.ShapeDtypeStruct((b, 4, N_CMP_PAD, NSA_DH), BF16),
        compiler_params=pltpu.CompilerParams(dimension_semantics=("arbitrary",), vmem_limit_bytes=VMEM_LIMIT),
    )(kc3, vc3, pe8, w1blk, w1f, w2p, *ctabs)


def _cmp_attn_kernel(q_ref, kc_ref, vc_ref, ovt_ref, o_ref, sb_ref):
    R = CMP_TQ
    r = pl.program_id(2)
    kc, vc = kc_ref[0, 0], vc_ref[0, 0]
    t_rows = r * R + lax.broadcasted_iota(jnp.int32, (R, N_CMP_PAD), 0)
    blk_end = lax.broadcasted_iota(jnp.int32, (R, N_CMP_PAD), 1) * CMP_STRIDE + (CMP_BLOCK - 1)
    valid = blk_end <= t_rows
    qs = jnp.concatenate([q_ref[0, :, j * NSA_DH:(j + 1) * NSA_DH] for j in range(NSA_HPG)], axis=0)
    sm = jnp.where(valid[None], _dot_nt(qs, kc).reshape(NSA_HPG, R, N_CMP_PAD), NEG_INF)
    e = jnp.exp2(sm - jnp.max(sm, axis=-1, keepdims=True))
    p = jnp.where(valid[None], e, 0.0) / jnp.sum(e, axis=-1, keepdims=True)
    o = _dot(p.reshape(NSA_HPG * R, N_CMP_PAD).astype(BF16), vc)
    for j in range(NSA_HPG):
        o_ref[0, :, j * NSA_DH:(j + 1) * NSA_DH] = o[j * R:(j + 1) * R]
    psum = jnp.sum(p, axis=0)
    imp = lax.dot_general(ovt_ref[...], psum, (((1,), (1,)), ((), ())), preferred_element_type=F32,
                          precision=lax.Precision.HIGHEST)
    cur = (r * R + lax.broadcasted_iota(jnp.int32, (N_SEL, R), 1)) // SEL_BLOCK
    bid = lax.broadcasted_iota(jnp.int32, (N_SEL, R), 0)
    forced = (bid == 0) | (bid == cur) | (bid == cur - 1)
    imp = jnp.where(forced, FORCE_SCORE, jnp.where(bid > cur, -FORCE_SCORE, imp))
    rank = jnp.zeros((N_SEL, R), F32)
    for m in range(N_SEL):
        a = imp[m:m + 1, :]
        before = jnp.where(bid > m, jnp.where(a >= imp, 1.0, 0.0), jnp.where(a > imp, 1.0, 0.0))
        rank = rank + before
    bias_t = jnp.where(rank < SEL_TOP_N, 0.0, NEG_INF)
    bias = bias_t.T
    sb_ref[0, 0] = jnp.concatenate([bias, jnp.zeros((R, NSA_DH - N_SEL), F32)], axis=1).astype(BF16)


def _cmp_attn(q3, kcvc, ovt):
    b = q3.shape[0]
    R = CMP_TQ
    gw = NSA_HPG * NSA_DH
    return pl.pallas_call(
        _cmp_attn_kernel,
        name="cmp_attn",
        grid=(b, NSA_GROUPS, SEQ // R),
        in_specs=[pl.BlockSpec((1, R, gw), lambda i, g, r: (i, r, g)),
                  pl.BlockSpec((1, 1, N_CMP_PAD, NSA_DH), lambda i, g, r: (i, g, 0, 0)),
                  pl.BlockSpec((1, 1, N_CMP_PAD, NSA_DH), lambda i, g, r: (i, NSA_GROUPS + g, 0, 0)),
                  _const_spec((N_SEL, N_CMP_PAD))],
        out_specs=[pl.BlockSpec((1, R, gw), lambda i, g, r: (i, r, g)),
                   pl.BlockSpec((1, 1, R, NSA_DH), lambda i, g, r: (i, g, r, 0))],
        out_shape=(jax.ShapeDtypeStruct((b, SEQ, BRANCH), F32),
                   jax.ShapeDtypeStruct((b, NSA_GROUPS, SEQ, NSA_DH), BF16)),
        compiler_params=pltpu.CompilerParams(dimension_semantics=("arbitrary",) * 3, vmem_limit_bytes=VMEM_LIMIT),
    )(q3, kcvc, kcvc, ovt)


def _sel_win_kernel(q_ref, kv_ref, sb_ref, acc_ref, qa_ref, m_ref):
    TQ, TK = ATT_TQ, ATT_TK
    M = NSA_HPG * TQ
    qi = pl.program_id(1)

    def chunk(rows, c, g):
        return kv_ref[0, rows, (2 * c + g) * 128:(2 * c + g + 1) * 128]

    for g in range(NSA_GROUPS):
        sb = sb_ref[0, g]
        for j in range(NSA_HPG):
            h = g * NSA_HPG + j
            qa_ref[g, j * TQ:(j + 1) * TQ, :] = jnp.concatenate([q_ref[0, :, h * NSA_DH:(h + 1) * NSA_DH], sb], axis=1)

    def online(idx, s, v, first=False):
        row_max = jnp.max(s, axis=-1, keepdims=True)
        if first:
            m_new = jnp.broadcast_to(row_max, (M, 128))
        else:
            m = m_ref[idx]
            m_new = jnp.maximum(m, row_max)
        p = jnp.exp2(s - pltpu.repeat(m_new, s.shape[1] // 128, axis=1))
        pv = _dot(p.astype(BF16), v)
        acc_ref[0, 0, idx] = pv if first else jnp.exp2(m - m_new) * acc_ref[0, 0, idx] + pv
        m_ref[idx] = m_new

    def far_tiles(k0, width):
        rows = pl.ds(pl.multiple_of(k0, TK), width)
        for g in range(NSA_GROUPS):
            online(g, _dot_nt(qa_ref[g], chunk(rows, 0, g)), chunk(rows, 1, g))

    def near_tile(t, first):
        rows = pl.ds(pl.multiple_of(t * TK, TK), TK)
        dist = (lax.broadcasted_iota(jnp.int32, (TQ, TK), 0) - lax.broadcasted_iota(jnp.int32, (TQ, TK), 1)
                + (qi - t) * TK)
        causal_bias = jnp.where(dist >= 0, 0.0, NEG_INF)
        window_bias = jnp.where(lax.bitcast_convert_type(dist, jnp.uint32) < WINDOW, 0.0, NEG_INF)

        def masked(s, bias):
            return (s.reshape(NSA_HPG, TQ, TK) + bias[None]).reshape(M, TK)

        for g in range(NSA_GROUPS):
            qa = qa_ref[g]
            online(g, masked(_dot_nt(qa, chunk(rows, 0, g)), causal_bias), chunk(rows, 1, g), first)
            online(NSA_GROUPS + g, masked(_dot_nt(qa, chunk(rows, 2, g)), window_bias), chunk(rows, 3, g), first)

    max_near = WINDOW // TK + 1
    for count in range(1, max_near + 1):
        @pl.when((qi == count - 1) if count < max_near else (qi >= count - 1))
        def _(count=count):
            for back in range(count):
                near_tile(qi - back, first=(back == 0))

    n_far = jnp.maximum(qi - WINDOW // TK, 0)

    def far_pair(i, carry):
        far_tiles(i * (2 * TK), 2 * TK)
        return carry

    lax.fori_loop(0, n_far // 2, far_pair, 0)

    @pl.when(n_far % 2 == 1)
    def _():
        far_tiles((n_far - 1) * TK, TK)


def _sel_win(q3, kv3, selbias):
    b = q3.shape[0]
    TQ = ATT_TQ
    M = NSA_HPG * TQ
    return pl.pallas_call(
        _sel_win_kernel,
        name="sel_win_attn",
        grid=(b, SEQ // TQ),
        in_specs=[pl.BlockSpec((1, TQ, BRANCH), lambda i, t: (i, t, 0)),
                  pl.BlockSpec((1, SEQ, KV_LANES), lambda i, t: (i, 0, 0)),
                  pl.BlockSpec((1, NSA_GROUPS, TQ, NSA_DH), lambda i, t: (i, 0, t, 0))],
        out_specs=pl.BlockSpec((1, 1, 2 * NSA_GROUPS, M, 128), lambda i, t: (i, t, 0, 0, 0)),
        out_shape=jax.ShapeDtypeStruct((b, SEQ // TQ, 2 * NSA_GROUPS, M, 128), F32),
        scratch_shapes=[pltpu.VMEM((NSA_GROUPS, M, 128), BF16), pltpu.VMEM((2 * NSA_GROUPS, M, 128), F32)],
        compiler_params=pltpu.CompilerParams(dimension_semantics=("arbitrary", "arbitrary"),
                                             vmem_limit_bytes=VMEM_LIMIT),
    )(q3, kv3, selbias)


def _retention_kernel(qkv_ref, zg_ref, intra_ref, qd_ref, kd_ref, cd_ref, gn_ref, y_ref, st_ref):
    C = RET_CHUNK
    NC = SEQ // C

    def bdot(a, b, ca, cb):
        return lax.dot_general(a, b, (((ca,), (cb,)), ((0,), (0,))), preferred_element_type=F32)

    for h in range(RET_HEADS):
        lanes = slice(h * RET_DK, (h + 1) * RET_DK)
        q3, k3, v3 = (qkv_ref[0, :, part * BRANCH + h * RET_DK:part * BRANCH + (h + 1) * RET_DK].reshape(NC, C, RET_DK)
                      for part in range(3))
        scores = bdot(q3, k3, 2, 2) * intra_ref[h][None]
        inner = bdot(scores.astype(BF16), v3, 2, 1)
        kdec = (k3.astype(F32) * kd_ref[h][None]).astype(BF16)
        kv = bdot(kdec, v3, 1, 1)
        cd = cd_ref[h, 0:1, :]
        state = jnp.zeros((RET_DK, RET_DK), F32)
        for c in range(NC):
            st_ref[c] = state.astype(BF16)
            state = state * cd + kv[c]
        qdec = (q3.astype(F32) * qd_ref[h][None]).astype(BF16)
        o = inner + bdot(qdec, st_ref[...], 2, 1)
        mu = jnp.mean(o, axis=-1, keepdims=True)
        d = o - mu
        var = jnp.mean(d * d, axis=-1, keepdims=True)
        o = (d * lax.rsqrt(var + EPS)).reshape(SEQ, RET_DK) * gn_ref[:, lanes]
        y_ref[0, :, lanes] = (o * jax.nn.silu(zg_ref[0, :, lanes])).astype(BF16)


def _retention(rq3, rg3, intra, qd, kd, cd, gn):
    b = rq3.shape[0]
    return pl.pallas_call(
        _retention_kernel,
        name="retention",
        grid=(b,),
        in_specs=[pl.BlockSpec((1, SEQ, 3 * BRANCH), lambda i: (i, 0, 0)),
                  pl.BlockSpec((1, SEQ, BRANCH), lambda i: (i, 0, 0)),
                  _const_spec((RET_HEADS, RET_CHUNK, RET_DK)), _const_spec((RET_HEADS, RET_CHUNK, RET_DK)),
                  _const_spec((RET_HEADS, RET_CHUNK, RET_DK)), _const_spec((RET_HEADS, 8, RET_DK)),
                  _const_spec((1, BRANCH))],
        out_specs=pl.BlockSpec((1, SEQ, BRANCH), lambda i: (i, 0, 0)),
        out_shape=jax.ShapeDtypeStruct((b, SEQ, BRANCH), BF16),
        scratch_shapes=[pltpu.VMEM((SEQ // RET_CHUNK, RET_DK, RET_DK), BF16)],
        compiler_params=pltpu.CompilerParams(dimension_semantics=("arbitrary",), vmem_limit_bytes=VMEM_LIMIT),
    )(rq3, rg3, intra, qd, kd, cd, gn)


def _merge_kernel(x_ref, g_ref, wg_ref, bg_ref, ya_ref, oc_ref, acc_ref, ng_ref, yc_ref, yd_ref, wb_ref, wo_ref, o_ref):
    T = MERGE_TILE
    TQ = ATT_TQ
    x = x_ref[...]
    h = _rms(x, g_ref[...]).astype(BF16)

    lane = lax.broadcasted_iota(jnp.int32, (TQ, 128), 1)

    def branch_out(br):
        rows = []
        for part in range(T // TQ):
            pairs = []
            for g in range(NSA_GROUPS):
                halves = []
                for j in range(NSA_HPG):
                    acc = acc_ref[part, 0, br * NSA_GROUPS + g, j * TQ:(j + 1) * TQ, :]
                    rolled = pltpu.roll(acc, NSA_DH, 1)
                    halves.append(acc * (1.0 / rolled) if j % 2 == 0 else rolled * (1.0 / acc))
                pairs += [jnp.where(lane < NSA_DH, halves[2 * c], halves[2 * c + 1]) for c in range(NSA_HPG // 2)]
            rows.append(jnp.concatenate(pairs, axis=1))
        return jnp.concatenate(rows, axis=0)

    def head_gates(br):
        return jnp.concatenate([jnp.broadcast_to(ng[:, 3 * hd + br:3 * hd + br + 1], (T, NSA_DH))
                                for hd in range(NSA_HEADS)], axis=1)

    W = 256

    def stage_dots(n, y, c):
        cols = slice(n * D_MODEL + c, n * D_MODEL + c + W)
        return _dot(h, wg_ref[:, cols]), _dot(y, wb_ref[n, :, c:c + W]), bg_ref[:, cols]

    merged = [None] * (D_MODEL // W)

    def stage_gate(c, dots):
        logits, proj, bias = dots
        term = proj * jax.nn.sigmoid(logits + bias)
        merged[c // W] = term if merged[c // W] is None else merged[c // W] + term

    pending = None

    def run_branch(n, y):
        nonlocal pending
        for c in range(0, D_MODEL, W):
            dots = stage_dots(n, y, c)
            if pending is not None:
                stage_gate(*pending)
            pending = (c, dots)

    ng = ng_ref[...]
    run_branch(0, ya_ref[...])
    yb = head_gates(0) * oc_ref[...] + head_gates(1) * branch_out(0)
    run_branch(2, yc_ref[...])
    yb = yb + head_gates(2) * branch_out(1)
    run_branch(3, yd_ref[...])
    run_branch(1, yb.astype(BF16))
    stage_gate(*pending)
    o_ref[...] = x + _dot(jnp.concatenate(merged, axis=1).astype(BF16), wo_ref[...])


def _merge(x2, g, wg, layer, bg, ya, oc, acc, ng, yc, yd, wb, wo):
    n = x2.shape[0]
    T = MERGE_TILE
    tok = lambda width: pl.BlockSpec((T, width), lambda i: (i, 0))
    return pl.pallas_call(
        _merge_kernel,
        name="merge",
        grid=(n // T,),
        in_specs=[tok(D_MODEL), _const_spec((1, D_MODEL)), _layer_spec(layer, (D_MODEL, N_BRANCH * D_MODEL)),
                  _const_spec((1, N_BRANCH * D_MODEL)), tok(BRANCH), tok(BRANCH),
                  pl.BlockSpec((T // ATT_TQ, 1) + acc.shape[2:], lambda i: (i, 0, 0, 0, 0)), tok(NG_PAD),
                  tok(BRANCH), tok(BRANCH),
                  _layer_spec(layer, (N_BRANCH, BRANCH, D_MODEL)), _layer_spec(layer, (D_MODEL, D_MODEL))],
        out_specs=tok(D_MODEL),
        out_shape=jax.ShapeDtypeStruct((n, D_MODEL), F32),
        compiler_params=pltpu.CompilerParams(dimension_semantics=("arbitrary",), vmem_limit_bytes=VMEM_LIMIT),
    )(x2, g, wg, bg, ya, oc, acc, ng, yc, yd, wb, wo)


FF_CHUNK = 256


def _mlp_kernel(x_ref, g_ref, wup_ref, cw_ref, wdn_ref, fg_ref, o_ref, ug_buf, act_buf, *, final_norm):
    T = MLP_TILE
    i = pl.program_id(0)

    @pl.when(i % (SEQ // T) == 0)
    def _():
        ug_buf[0:CONV_HALO, :] = jnp.zeros((CONV_HALO, D_FF), F32)

    x = x_ref[...]
    h = _rms(x, g_ref[...]).astype(BF16)
    for c in range(D_FF // FF_CHUNK):
        cols = slice(c * FF_CHUNK, (c + 1) * FF_CHUNK)
        ug_buf[CONV_HALO:CONV_HALO + T, cols] = _dot(h, wup_ref[:, cols])
        conv = cw_ref[CONV_TAPS - 1:CONV_TAPS, cols] * ug_buf[CONV_HALO:CONV_HALO + T, cols]
        for k in range(1, CONV_TAPS):
            conv = conv + cw_ref[CONV_TAPS - 1 - k:CONV_TAPS - k, cols] * ug_buf[CONV_HALO - k:CONV_HALO - k + T, cols]
        val = _dot(h, wup_ref[:, D_FF + c * FF_CHUNK:D_FF + (c + 1) * FF_CHUNK])
        act_buf[:, cols] = (jax.nn.silu(conv) * val).astype(BF16)
    ug_buf[0:CONV_HALO, :] = ug_buf[T:T + CONV_HALO, :]
    y = x + _dot(act_buf[...], wdn_ref[...])
    if final_norm:
        y = _rms(y, fg_ref[...])
    o_ref[...] = y


def _mlp(x2, g, wup, layer, cw, wdn, fg, final_norm):
    n = x2.shape[0]
    T = MLP_TILE
    tok = pl.BlockSpec((T, D_MODEL), lambda i: (i, 0))
    return pl.pallas_call(
        functools.partial(_mlp_kernel, final_norm=final_norm),
        name="mlp",
        grid=(n // T,),
        in_specs=[tok, _const_spec((1, D_MODEL)), _layer_spec(layer, (D_MODEL, 2 * D_FF)), _const_spec((CONV_TAPS, D_FF)),
                  _layer_spec(layer, (D_FF, D_MODEL)), _const_spec((1, D_MODEL))],
        out_specs=tok,
        out_shape=jax.ShapeDtypeStruct((n, D_MODEL), F32),
        scratch_shapes=[pltpu.VMEM((CONV_HALO + T, D_FF), F32), pltpu.VMEM((T, D_FF), BF16)],
        compiler_params=pltpu.CompilerParams(dimension_semantics=("arbitrary",), vmem_limit_bytes=VMEM_LIMIT),
    )(x2, g, wup, cw, wdn, fg)


def _rope_angles(pos, half, theta):
    inv_freq = np.power(np.float32(theta), -np.arange(half, dtype=np.float32) / np.float32(half))
    ang = pos.astype(F32)[:, None] * jnp.asarray(inv_freq)[None, :]
    return jnp.cos(ang), jnp.sin(ang)


def _nsa_rope_tables(pos):
    half = ROPE_DIM // 2
    cos, sin = _rope_angles(pos, half, ROPE_THETA)
    n = pos.shape[0]
    ones = jnp.ones((n, NSA_DH - ROPE_DIM), F32)
    zeros = jnp.zeros((n, NSA_DH - ROPE_DIM), F32)
    zh = jnp.zeros((n, half), F32)
    c = jnp.concatenate([cos, cos, ones], axis=1)
    sa = jnp.concatenate([-sin, zh, zeros], axis=1)
    sb = jnp.concatenate([zh, sin, zeros], axis=1)
    return tuple(jnp.tile(t, (1, 128 // NSA_DH)) for t in (c, sa, sb))


def _ret_rope_tables(pos):
    half = RET_DK // 2
    cos, sin = _rope_angles(pos, half, RET_THETA)
    return jnp.concatenate([cos, cos], axis=1), jnp.concatenate([-sin, sin], axis=1)


def _retention_decay_tables():
    H, C = RET_HEADS, RET_CHUNK
    log_g = np.log1p(-np.exp2(-5.0 - np.arange(H))).astype(np.float32)
    n = np.arange(C, dtype=np.float32)
    diff = n[:, None] - n[None, :]
    intra = np.where(diff >= 0, np.exp(np.maximum(diff, 0.0)[None] * log_g[:, None, None]), 0.0).astype(np.float32)
    q_dec = np.exp((n[None, :] + 1.0) * log_g[:, None]).astype(np.float32)
    k_dec = np.exp((C - 1.0 - n[None, :]) * log_g[:, None]).astype(np.float32)
    c_dec = np.exp(C * log_g).astype(np.float32)
    qd = np.broadcast_to(q_dec[:, :, None], (H, C, RET_DK))
    kd = np.broadcast_to(k_dec[:, :, None], (H, C, RET_DK))
    cd = np.broadcast_to(c_dec[:, None, None], (H, 8, RET_DK))
    return tuple(jnp.asarray(np.ascontiguousarray(t)) for t in (intra, qd, kd, cd))


def _overlap_t():
    cs0 = np.arange(N_CMP_PAD) * CMP_STRIDE
    ss0 = np.arange(N_SEL) * SEL_BLOCK
    ov = np.clip(np.minimum(cs0[:, None] + CMP_BLOCK, ss0[None, :] + SEL_BLOCK)
                 - np.maximum(cs0[:, None], ss0[None, :]), 0, None)
    ov = (ov / CMP_BLOCK).astype(np.float32)
    ov[N_CMP:, :] = 0.0
    return jnp.asarray(np.ascontiguousarray(ov.T))


def _block_onehot():
    e = np.zeros((SEQ, NSA_DH), np.float32)
    e[np.arange(SEQ), np.arange(SEQ) // SEL_BLOCK] = 1.0
    return jnp.asarray(e, dtype=BF16)


def _layer(x2, batch, layer, p, wts, tabs, final_g, final_norm):
    n = x2.shape[0]
    ya, q, kc_in, vc_in, kv2, ng, rq, rg, yd = _inproj(
        x2, p["norm1_g"][None, :], wts["main_a"], wts["main_ng"], wts["main_b"], layer, p["sc_conv"], tabs["tok"], tabs["onehot"], wts["pool"],
        p["pool_scale"][None, :])

    w1 = jnp.stack([p["cmp_w1_k"], p["cmp_w1_v"]])
    wcat = jnp.concatenate([w1[:, :CMP_STRIDE], w1[:, CMP_STRIDE:]], axis=3)
    zero = jnp.zeros_like(wcat)
    w1blk = jnp.concatenate([jnp.concatenate([wcat, zero], axis=3), jnp.concatenate([zero, wcat], axis=3)],
                            axis=2).astype(BF16)
    w1f = w1.reshape(2, CMP_BLOCK * NSA_DH, NSA_DH)
    w1f = jnp.concatenate([w1f, jnp.zeros_like(w1f)], axis=2).astype(BF16)
    pe8 = jnp.broadcast_to(p["cmp_pe"].reshape(1, -1), (8, CMP_BLOCK * NSA_DH)).astype(BF16)
    w2 = jnp.stack([p["cmp_w2_k"], p["cmp_w2_v"]])
    w2p = jnp.pad(w2, ((0, 0), (0, 128 - NSA_DH), (0, 128 - NSA_DH))).astype(BF16)
    kcvc = _compress(kc_in.reshape(batch, SEQ, 128), vc_in.reshape(batch, SEQ, 128), pe8, w1blk, w1f, w2p,
                     tabs["cmp"])

    q3 = q.reshape(batch, SEQ, BRANCH)
    ocmp, selbias = _cmp_attn(q3, kcvc, tabs["ovt"])
    acc = _sel_win(q3, kv2.reshape(batch, SEQ, KV_LANES), selbias)

    yc = _retention(rq.reshape(batch, SEQ, 3 * BRANCH), rg.reshape(batch, SEQ, BRANCH), *tabs["ret"],
                    p["ret_gn_g"][None, :])

    x2 = _merge(x2, p["norm1_g"][None, :], wts["gate"], layer, p["b_gate"][None, :], ya, ocmp.reshape(n, BRANCH),
                acc.reshape((n // ATT_TQ, 1) + acc.shape[2:]), ng, yc.reshape(n, BRANCH), yd,
                wts["branch"], wts["o"])
    return _mlp(x2, p["norm2_g"][None, :], wts["up"], layer, p["ffn_conv"], wts["down"], final_g[None, :], final_norm)


def kernel(x, norm1_g, w_in, b_gate, sc_conv, cmp_pe, cmp_w1_k, cmp_w2_k, cmp_w1_v, cmp_w2_v, ret_gn_g, pool_w, pool_scale, w_branch, w_o, norm2_g, w_up, ffn_conv, w_down, final_norm_g):
    batch, seq, d = x.shape
    assert seq == SEQ and d == D_MODEL
    depth = w_in.shape[0]
    small = dict(norm1_g=norm1_g, b_gate=b_gate, sc_conv=sc_conv, cmp_pe=cmp_pe, cmp_w1_k=cmp_w1_k,
                 cmp_w2_k=cmp_w2_k, cmp_w1_v=cmp_w1_v, cmp_w2_v=cmp_w2_v, ret_gn_g=ret_gn_g, pool_scale=pool_scale,
                 norm2_g=norm2_g, ffn_conv=ffn_conv)
    n_main = NG_OFF + NG_COLS
    w_in16 = w_in.astype(BF16)
    wts = dict(
        main_a=w_in16[:, :, :NG_OFF],
        main_ng=jnp.pad(w_in16[:, :, NG_OFF:n_main], ((0, 0), (0, 0), (0, NG_PAD - NG_COLS))),
        main_b=w_in16[:, :, n_main:GATE_OFF_ORIG],
        gate=w_in16[:, :, GATE_OFF_ORIG:], pool=pool_w.astype(BF16), branch=w_branch.astype(BF16),
        o=w_o.astype(BF16), up=w_up.astype(BF16), down=w_down.astype(BF16))
    pos = jnp.arange(SEQ)
    cmp_end = jnp.asarray(np.arange(N_CMP_PAD) * CMP_STRIDE + CMP_BLOCK - 1)
    tabs = dict(tok=_nsa_rope_tables(pos) + _ret_rope_tables(pos), cmp=_nsa_rope_tables(cmp_end),
                ret=_retention_decay_tables(), ovt=_overlap_t(), onehot=_block_onehot())
    x2 = x.reshape(batch * seq, d)
    for l in range(depth):
        p = {k: v[l] for k, v in small.items()}
        x2 = _layer(x2, batch, l, p, wts, tabs, final_norm_g, final_norm=(l == depth - 1))
    return x2.reshape(batch, seq, d)
```

```python
import functools

import numpy as np
import jax
import jax.numpy as jnp
from jax import lax
from jax.experimental import pallas as pl
from jax.experimental.pallas import tpu as pltpu

F32 = jnp.float32
BF16 = jnp.bfloat16

D_MODEL = 1024
SEQ = 2048
BRANCH = 512
N_BRANCH = 4
CONV_TAPS = 3
NSA_HEADS = 8
NSA_GROUPS = 2
NSA_HPG = NSA_HEADS // NSA_GROUPS
NSA_DH = BRANCH // NSA_HEADS
CMP_BLOCK = 32
CMP_STRIDE = 16
N_CMP = (SEQ - CMP_BLOCK) // CMP_STRIDE + 1
N_CMP_PAD = SEQ // CMP_STRIDE
SEL_BLOCK = 64
N_SEL = SEQ // SEL_BLOCK
SEL_TOP_N = 16
WINDOW = 512
ROPE_THETA = 500000.0
ROPE_DIM = NSA_DH // 4
FORCE_SCORE = 1.0e4
RET_HEADS = 4
RET_DK = BRANCH // RET_HEADS
RET_CHUNK = 128
RET_THETA = 10000.0
POOL_WINDOWS = (2, 4, 8, 16)
POOL_GROUP_DIM = BRANCH // len(POOL_WINDOWS)
D_FF = 2816
EPS = 1e-6
NEG_INF = -1e30
LOG2_E = 1.4426950408889634
Q_SCALE = NSA_DH ** -0.5 * LOG2_E

SC_OFF = 0
Q_OFF = 3 * BRANCH
KV_OFF = Q_OFF + BRANCH
NG_OFF = KV_OFF + 3 * 2 * NSA_GROUPS * NSA_DH
NG_COLS = 3 * NSA_HEADS
NG_PAD = 128
RET_OFF = NG_OFF + NG_PAD
POOL_OFF = RET_OFF + 4 * BRANCH
MAIN_COLS = POOL_OFF + BRANCH
GATE_OFF_ORIG = NG_OFF + NG_COLS + 4 * BRANCH + BRANCH

IN_TILE = 1024
MERGE_TILE = 512
MLP_TILE = 1024
CONV_HALO = 8
POOL_HALO = 16
ATT_TQ = 256
ATT_TK = 256
KV_LANES = 4 * NSA_GROUPS * 128
CMP_TQ = 2048
VMEM_LIMIT = 56 * 1024 * 1024


def _const_spec(shape):
    n = len(shape)
    return pl.BlockSpec(shape, lambda *_: (0,) * n, pipeline_mode=pl.Buffered(1))


def _layer_spec(layer, shape):
    n = len(shape)
    return pl.BlockSpec((None,) + tuple(shape), lambda *_: (layer,) + (0,) * n, pipeline_mode=pl.Buffered(1))


def _rms(x, g):
    ms = jnp.mean(x * x, axis=-1, keepdims=True)
    return x * lax.rsqrt(ms + EPS) * g


def _dot(a, b):
    return jnp.dot(a, b, preferred_element_type=F32)


def _dot_nt(a, b):
    return lax.dot_general(a, b, (((1,), (1,)), ((), ())), preferred_element_type=F32)


def _inproj_kernel(x_ref, g_ref, wa_ref, wn_ref, wb_ref, scw_ref, nc_ref, nsa_ref, nsb_ref, rc_ref, rs_ref, oh_ref, pw_ref, ps_ref,
                   ya_ref, q_ref, kc_ref, vc_ref, kv2_ref, ng_ref, rq_ref, rg_ref, yd_ref, ch_buf, u_buf):
    T = IN_TILE
    tiles_per_seq = SEQ // T
    i = pl.program_id(0)

    @pl.when(i % tiles_per_seq == 0)
    def _():
        ch_buf[0:CONV_HALO, :] = jnp.zeros((CONV_HALO, BRANCH), F32)
        u_buf[0:POOL_HALO, :] = jnp.zeros((POOL_HALO, BRANCH), F32)

    h = _rms(x_ref[...], g_ref[...]).astype(BF16)

    def proj(off, width):
        if off < NG_OFF:
            return _dot(h, wa_ref[:, off:off + width])
        if off == NG_OFF:
            return _dot(h, wn_ref[...])
        return _dot(h, wb_ref[:, off - RET_OFF:off - RET_OFF + width])

    nc, nsa, nsb = nc_ref[...], nsa_ref[...], nsb_ref[...]
    rc, rs = rc_ref[...], rs_ref[...]
    W = 256
    STAGE_LOOKAHEAD = 2

    def rope_nsa(x):
        return x * nc + pltpu.roll(x, 128 - ROPE_DIM // 2, 1) * nsa + pltpu.roll(x, ROPE_DIM // 2, 1) * nsb

    def rope_ret(x):
        return x * rc + pltpu.roll(x, RET_DK // 2, 1) * rs

    def short_conv(c0, zs):
        zb, zc, zh = zs
        cols = slice(c0, c0 + W)
        ch_buf[CONV_HALO:CONV_HALO + T, cols] = zc * zh
        conv = scw_ref[CONV_TAPS - 1:CONV_TAPS, cols] * ch_buf[CONV_HALO:CONV_HALO + T, cols]
        for k in range(1, CONV_TAPS):
            conv = conv + scw_ref[CONV_TAPS - 1 - k:CONV_TAPS - k, cols] * ch_buf[CONV_HALO - k:CONV_HALO - k + T, cols]
        ya_ref[:, cols] = (zb * conv).astype(BF16)
        ch_buf[0:CONV_HALO, cols] = ch_buf[T:T + CONV_HALO, cols]

    def pooling(c0, zs):
        (zu,) = zs
        u_buf[POOL_HALO:POOL_HALO + T, c0:c0 + W] = zu
        pos = (i % tiles_per_seq) * T + lax.broadcasted_iota(jnp.int32, (T, 1), 0)
        for gi in range(c0 // POOL_GROUP_DIM, (c0 + W) // POOL_GROUP_DIM):
            win = POOL_WINDOWS[gi]
            g0, g1 = gi * POOL_GROUP_DIM, (gi + 1) * POOL_GROUP_DIM
            s = u_buf[:, g0:g1]
            shift = 1
            while shift < win:
                s = s + pltpu.roll(s, shift, 0)
                shift *= 2
            cnt = jnp.minimum(pos + 1, win).astype(F32)
            pooled = s[POOL_HALO:, :] / cnt - zu[:, g0 - c0:g1 - c0]
            y = _dot(pooled.astype(BF16), pw_ref[gi]) * ps_ref[:, g0:g1]
            yd_ref[:, g0:g1] = y.astype(BF16)
        u_buf[0:POOL_HALO, c0:c0 + W] = u_buf[T:T + POOL_HALO, c0:c0 + W]

    def attn_q(c0, zs):
        for c in range(W // 128):
            lanes = slice(c0 + c * 128, c0 + (c + 1) * 128)
            q_ref[:, lanes] = (rope_nsa(zs[0][:, c * 128:(c + 1) * 128]) * Q_SCALE).astype(BF16)

    def cmp_kv(c0, zs):
        kc_ref[...] = zs[0][:, 0:128]
        vc_ref[...] = zs[0][:, 128:256]

    def attn_kv(c0, zs):
        branch = c0 // W - 1
        key_fill = oh_ref[...] if branch == 0 else jnp.zeros((T, NSA_DH), BF16)
        pieces = ((rope_nsa(zs[0][:, 0:128]).astype(BF16), key_fill),
                  (zs[0][:, 128:256].astype(BF16), jnp.ones((T, NSA_DH), BF16)))
        for c, (piece, fill) in enumerate(pieces):
            for g in range(NSA_GROUPS):
                lane0 = (2 * (2 * branch + c) + g) * 128
                kv2_ref[:, lane0:lane0 + 128] = jnp.concatenate([piece[:, g * NSA_DH:(g + 1) * NSA_DH], fill], axis=1)

    def attn_gates(c0, zs):
        ng_ref[...] = jax.nn.sigmoid(zs[0])

    def ret_q(c0, zs):
        for c in range(W // 128):
            rq_ref[:, c0 + c * 128:c0 + (c + 1) * 128] = rope_ret(zs[0][:, c * 128:(c + 1) * 128]).astype(BF16)

    def ret_k(c0, zs):
        for c in range(W // 128):
            rq_ref[:, BRANCH + c0 + c * 128:BRANCH + c0 + (c + 1) * 128] = (
                rope_ret(zs[0][:, c * 128:(c + 1) * 128]) * (RET_DK ** -0.5)).astype(BF16)

    def ret_v(c0, zs):
        rq_ref[:, 2 * BRANCH + c0:2 * BRANCH + c0 + W] = zs[0].astype(BF16)

    def ret_gate(c0, zs):
        rg_ref[:, c0:c0 + W] = zs[0]

    stages = []
    for c0 in range(0, BRANCH, W):
        stages.append(((SC_OFF + c0, SC_OFF + BRANCH + c0, SC_OFF + 2 * BRANCH + c0), W, c0, short_conv))
    for c0 in range(0, BRANCH, W):
        stages.append(((POOL_OFF + c0,), W, c0, pooling))
    for c0 in range(0, BRANCH, W):
        stages.append(((Q_OFF + c0,), W, c0, attn_q))
    stages.append(((KV_OFF,), W, 0, cmp_kv))
    stages.append(((KV_OFF + W,), W, W, attn_kv))
    stages.append(((KV_OFF + 2 * W,), W, 2 * W, attn_kv))
    stages.append(((NG_OFF,), NG_PAD, 0, attn_gates))
    for consumer, base in ((ret_q, RET_OFF), (ret_k, RET_OFF + BRANCH), (ret_v, RET_OFF + 2 * BRANCH),
                           (ret_gate, RET_OFF + 3 * BRANCH)):
        for c0 in range(0, BRANCH, W):
            stages.append(((base + c0,), W, c0, consumer))
    queue = []
    for offs, width, c0, consumer in stages:
        queue.append((consumer, c0, tuple(proj(off, width) for off in offs)))
        if len(queue) > STAGE_LOOKAHEAD:
            ready, ready_c0, ready_zs = queue.pop(0)
            ready(ready_c0, ready_zs)
    for ready, ready_c0, ready_zs in queue:
        ready(ready_c0, ready_zs)


def _inproj(x2, g, w_a, w_ng, w_b, layer, scw, tabs, onehot, pool_w, pool_scale):
    n = x2.shape[0]
    T = IN_TILE
    tok = lambda width: pl.BlockSpec((T, width), lambda i: (i, 0))
    tab = pl.BlockSpec((T, 128), lambda i: (i % (SEQ // T), 0))
    out_shapes = (
        jax.ShapeDtypeStruct((n, BRANCH), BF16),
        jax.ShapeDtypeStruct((n, BRANCH), BF16),
        jax.ShapeDtypeStruct((n, 128), F32),
        jax.ShapeDtypeStruct((n, 128), F32),
        jax.ShapeDtypeStruct((n, KV_LANES), BF16),
        jax.ShapeDtypeStruct((n, NG_PAD), F32),
        jax.ShapeDtypeStruct((n, 3 * BRANCH), BF16),
        jax.ShapeDtypeStruct((n, BRANCH), F32),
        jax.ShapeDtypeStruct((n, BRANCH), BF16),
    )
    return pl.pallas_call(
        _inproj_kernel,
        name="inproj",
        grid=(n // T,),
        in_specs=[tok(D_MODEL), _const_spec((1, D_MODEL)), _layer_spec(layer, (D_MODEL, NG_OFF)),
                  _layer_spec(layer, (D_MODEL, NG_PAD)), _layer_spec(layer, (D_MODEL, MAIN_COLS - RET_OFF)),
                  _const_spec((CONV_TAPS, BRANCH)), tab, tab, tab, tab, tab,
                  pl.BlockSpec((T, NSA_DH), lambda i: (i % (SEQ // T), 0)),
                  _layer_spec(layer, (len(POOL_WINDOWS), POOL_GROUP_DIM, POOL_GROUP_DIM)), _const_spec((1, BRANCH))],
        out_specs=[tok(BRANCH), tok(BRANCH), tok(128), tok(128), tok(KV_LANES), tok(NG_PAD), tok(3 * BRANCH), tok(BRANCH),
                   tok(BRANCH)],
        out_shape=out_shapes,
        scratch_shapes=[pltpu.VMEM((CONV_HALO + T, BRANCH), F32), pltpu.VMEM((POOL_HALO + T, BRANCH), F32)],
        compiler_params=pltpu.CompilerParams(dimension_semantics=("arbitrary",), vmem_limit_bytes=VMEM_LIMIT),
    )(x2, g, w_a, w_ng, w_b, scw, *tabs, onehot, pool_w, pool_scale)


def _compress_blocks(kc_ref, vc_ref, pe_ref, w1_ref, w1f_ref, w2_ref, cc_ref, csa_ref, csb_ref, store):
    for kv, x_ref in enumerate((kc_ref, vc_ref)):
        acc = jnp.zeros((N_CMP_PAD, 4 * NSA_DH), F32)
        for l in range(CMP_STRIDE):
            x = x_ref[0, pl.ds(l, N_CMP_PAD, stride=CMP_STRIDE), :].astype(BF16)
            acc = acc + _dot(x, w1_ref[kv, l])
        bias = _dot(pe_ref[...], w1f_ref[kv])[0:1, :]
        for g in range(NSA_GROUPS):
            a = acc[:, g * 128:(g + 1) * 128]
            hid = jax.nn.gelu(a + pltpu.roll(pltpu.roll(a, N_CMP_PAD - 1, 0), NSA_DH, 1) + bias)
            y = _dot(hid.astype(BF16), w2_ref[kv])
            if kv == 0:
                y = (y * cc_ref[...] + pltpu.roll(y, 128 - ROPE_DIM // 2, 1) * csa_ref[...]
                     + pltpu.roll(y, ROPE_DIM // 2, 1) * csb_ref[...])
            store(kv * NSA_GROUPS + g, y[:, 0:NSA_DH].astype(BF16))


def _cmp_attn_kernel(q_ref, kin_ref, vin_ref, pe_ref, w1_ref, w1f_ref, w2_ref, cc_ref, csa_ref, csb_ref, ovt_ref,
                     o_ref, sb_ref, kcvc_ref):
    R = CMP_TQ
    g = pl.program_id(1)
    r = pl.program_id(2)

    @pl.when((g == 0) & (r == 0))
    def _():
        def store(idx, y):
            kcvc_ref[idx] = y
        _compress_blocks(kin_ref, vin_ref, pe_ref, w1_ref, w1f_ref, w2_ref, cc_ref, csa_ref, csb_ref, store)

    kc, vc = kcvc_ref[g], kcvc_ref[NSA_GROUPS + g]
    t_rows = r * R + lax.broadcasted_iota(jnp.int32, (R, N_CMP_PAD), 0)
    blk_end = lax.broadcasted_iota(jnp.int32, (R, N_CMP_PAD), 1) * CMP_STRIDE + (CMP_BLOCK - 1)
    valid = blk_end <= t_rows
    qs = jnp.concatenate([q_ref[0, :, j * NSA_DH:(j + 1) * NSA_DH] for j in range(NSA_HPG)], axis=0)
    sm = jnp.where(valid[None], _dot_nt(qs, kc).reshape(NSA_HPG, R, N_CMP_PAD), NEG_INF)
    e = jnp.exp2(sm - jnp.max(sm, axis=-1, keepdims=True))
    p = jnp.where(valid[None], e, 0.0) / jnp.sum(e, axis=-1, keepdims=True)
    o = _dot(p.reshape(NSA_HPG * R, N_CMP_PAD).astype(BF16), vc)
    for j in range(NSA_HPG):
        o_ref[0, :, j * NSA_DH:(j + 1) * NSA_DH] = o[j * R:(j + 1) * R]
    psum = jnp.sum(p, axis=0)
    imp = lax.dot_general(ovt_ref[...], psum, (((1,), (1,)), ((), ())), preferred_element_type=F32,
                          precision=lax.Precision.HIGHEST)
    cur = (r * R + lax.broadcasted_iota(jnp.int32, (N_SEL, R), 1)) // SEL_BLOCK
    bid = lax.broadcasted_iota(jnp.int32, (N_SEL, R), 0)
    forced = (bid == 0) | (bid == cur) | (bid == cur - 1)
    imp = jnp.where(forced, FORCE_SCORE, jnp.where(bid > cur, -FORCE_SCORE, imp))
    rank = jnp.zeros((N_SEL, R), F32)
    for m in range(N_SEL):
        a = imp[m:m + 1, :]
        before = jnp.where(bid > m, jnp.where(a >= imp, 1.0, 0.0), jnp.where(a > imp, 1.0, 0.0))
        rank = rank + before
    bias_t = jnp.where(rank < SEL_TOP_N, 0.0, NEG_INF)
    bias = bias_t.T
    sb_ref[0, 0] = jnp.concatenate([bias, jnp.zeros((R, NSA_DH - N_SEL), F32)], axis=1).astype(BF16)


def _cmp_attn(q3, kc3, vc3, pe8, w1blk, w1f, w2p, ctabs, ovt):
    b = q3.shape[0]
    R = CMP_TQ
    gw = NSA_HPG * NSA_DH
    kvspec = pl.BlockSpec((1, SEQ, 128), lambda i, g, r: (i, 0, 0))
    return pl.pallas_call(
        _cmp_attn_kernel,
        name="cmp_attn",
        grid=(b, NSA_GROUPS, SEQ // R),
        in_specs=[pl.BlockSpec((1, R, gw), lambda i, g, r: (i, r, g)), kvspec, kvspec,
                  _const_spec((8, CMP_BLOCK * NSA_DH)), _const_spec((2, CMP_STRIDE, 128, 4 * NSA_DH)),
                  _const_spec((2, CMP_BLOCK * NSA_DH, 128)), _const_spec((2, 128, 128)),
                  _const_spec((N_CMP_PAD, 128)), _const_spec((N_CMP_PAD, 128)), _const_spec((N_CMP_PAD, 128)),
                  _const_spec((N_SEL, N_CMP_PAD))],
        out_specs=[pl.BlockSpec((1, R, gw), lambda i, g, r: (i, r, g)),
                   pl.BlockSpec((1, 1, R, NSA_DH), lambda i, g, r: (i, g, r, 0))],
        out_shape=(jax.ShapeDtypeStruct((b, SEQ, BRANCH), F32),
                   jax.ShapeDtypeStruct((b, NSA_GROUPS, SEQ, NSA_DH), BF16)),
        scratch_shapes=[pltpu.VMEM((2 * NSA_GROUPS, N_CMP_PAD, NSA_DH), BF16)],
        compiler_params=pltpu.CompilerParams(dimension_semantics=("arbitrary",) * 3, vmem_limit_bytes=VMEM_LIMIT),
    )(q3, kc3, vc3, pe8, w1blk, w1f, w2p, *ctabs, ovt)


def _sel_win_kernel(q_ref, kv_ref, sb_ref, acc_ref, qa_ref, m_ref):
    TQ, TK = ATT_TQ, ATT_TK
    M = NSA_HPG * TQ
    qi = pl.program_id(1)

    def chunk(rows, c, g):
        return kv_ref[0, rows, (2 * c + g) * 128:(2 * c + g + 1) * 128]

    for g in range(NSA_GROUPS):
        sb = sb_ref[0, g]
        for j in range(NSA_HPG):
            h = g * NSA_HPG + j
            qa_ref[g, j * TQ:(j + 1) * TQ, :] = jnp.concatenate([q_ref[0, :, h * NSA_DH:(h + 1) * NSA_DH], sb], axis=1)

    def online(idx, s, v, first=False):
        row_max = jnp.max(s, axis=-1, keepdims=True)
        if first:
            m_new = jnp.broadcast_to(row_max, (M, 128))
        else:
            m = m_ref[idx]
            m_new = jnp.maximum(m, row_max)
        p = jnp.exp2(s - pltpu.repeat(m_new, s.shape[1] // 128, axis=1))
        pv = _dot(p.astype(BF16), v)
        acc_ref[0, 0, idx] = pv if first else jnp.exp2(m - m_new) * acc_ref[0, 0, idx] + pv
        m_ref[idx] = m_new

    def far_tiles(k0, width):
        rows = pl.ds(pl.multiple_of(k0, TK), width)
        for g in range(NSA_GROUPS):
            online(g, _dot_nt(qa_ref[g], chunk(rows, 0, g)), chunk(rows, 1, g))

    def near_tile(t, first):
        rows = pl.ds(pl.multiple_of(t * TK, TK), TK)
        dist = (lax.broadcasted_iota(jnp.int32, (TQ, TK), 0) - lax.broadcasted_iota(jnp.int32, (TQ, TK), 1)
                + (qi - t) * TK)
        causal_bias = jnp.where(dist >= 0, 0.0, NEG_INF)
        window_bias = jnp.where(lax.bitcast_convert_type(dist, jnp.uint32) < WINDOW, 0.0, NEG_INF)

        def masked(s, bias):
            return (s.reshape(NSA_HPG, TQ, TK) + bias[None]).reshape(M, TK)

        for g in range(NSA_GROUPS):
            qa = qa_ref[g]
            online(g, masked(_dot_nt(qa, chunk(rows, 0, g)), causal_bias), chunk(rows, 1, g), first)
            online(NSA_GROUPS + g, masked(_dot_nt(qa, chunk(rows, 2, g)), window_bias), chunk(rows, 3, g), first)

    max_near = WINDOW // TK + 1
    for count in range(1, max_near + 1):
        @pl.when((qi == count - 1) if count < max_near else (qi >= count - 1))
        def _(count=count):
            for back in range(count):
                near_tile(qi - back, first=(back == 0))

    n_far = jnp.maximum(qi - WINDOW // TK, 0)

    def far_pair(i, carry):
        far_tiles(i * (2 * TK), 2 * TK)
        return carry

    lax.fori_loop(0, n_far // 2, far_pair, 0)

    @pl.when(n_far % 2 == 1)
    def _():
        far_tiles((n_far - 1) * TK, TK)


def _sel_win(q3, kv3, selbias):
    b = q3.shape[0]
    TQ = ATT_TQ
    M = NSA_HPG * TQ
    return pl.pallas_call(
        _sel_win_kernel,
        name="sel_win_attn",
        grid=(b, SEQ // TQ),
        in_specs=[pl.BlockSpec((1, TQ, BRANCH), lambda i, t: (i, t, 0)),
                  pl.BlockSpec((1, SEQ, KV_LANES), lambda i, t: (i, 0, 0)),
                  pl.BlockSpec((1, NSA_GROUPS, TQ, NSA_DH), lambda i, t: (i, 0, t, 0))],
        out_specs=pl.BlockSpec((1, 1, 2 * NSA_GROUPS, M, 128), lambda i, t: (i, t, 0, 0, 0)),
        out_shape=jax.ShapeDtypeStruct((b, SEQ // TQ, 2 * NSA_GROUPS, M, 128), F32),
        scratch_shapes=[pltpu.VMEM((NSA_GROUPS, M, 128), BF16), pltpu.VMEM((2 * NSA_GROUPS, M, 128), F32)],
        compiler_params=pltpu.CompilerParams(dimension_semantics=("arbitrary", "arbitrary"),
                                             vmem_limit_bytes=VMEM_LIMIT),
    )(q3, kv3, selbias)


def _retention_kernel(qkv_ref, zg_ref, intra_ref, qd_ref, kd_ref, cd_ref, gn_ref, y_ref, st_ref):
    C = RET_CHUNK
    NC = SEQ // C

    def bdot(a, b, ca, cb):
        return lax.dot_general(a, b, (((ca,), (cb,)), ((0,), (0,))), preferred_element_type=F32)

    for h in range(RET_HEADS):
        lanes = slice(h * RET_DK, (h + 1) * RET_DK)
        q3, k3, v3 = (qkv_ref[0, :, part * BRANCH + h * RET_DK:part * BRANCH + (h + 1) * RET_DK].reshape(NC, C, RET_DK)
                      for part in range(3))
        scores = bdot(q3, k3, 2, 2) * intra_ref[h][None]
        inner = bdot(scores.astype(BF16), v3, 2, 1)
        kdec = (k3.astype(F32) * kd_ref[h][None]).astype(BF16)
        kv = bdot(kdec, v3, 1, 1)
        cd = cd_ref[h, 0:1, :]
        state = jnp.zeros((RET_DK, RET_DK), F32)
        for c in range(NC):
            st_ref[c] = state.astype(BF16)
            state = state * cd + kv[c]
        qdec = (q3.astype(F32) * qd_ref[h][None]).astype(BF16)
        o = inner + bdot(qdec, st_ref[...], 2, 1)
        mu = jnp.mean(o, axis=-1, keepdims=True)
        d = o - mu
        var = jnp.mean(d * d, axis=-1, keepdims=True)
        o = (d * lax.rsqrt(var + EPS)).reshape(SEQ, RET_DK) * gn_ref[:, lanes]
        y_ref[0, :, lanes] = (o * jax.nn.silu(zg_ref[0, :, lanes])).astype(BF16)


def _retention(rq3, rg3, intra, qd, kd, cd, gn):
    b = rq3.shape[0]
    return pl.pallas_call(
        _retention_kernel,
        name="retention",
        grid=(b,),
        in_specs=[pl.BlockSpec((1, SEQ, 3 * BRANCH), lambda i: (i, 0, 0)),
                  pl.BlockSpec((1, SEQ, BRANCH), lambda i: (i, 0, 0)),
                  _const_spec((RET_HEADS, RET_CHUNK, RET_DK)), _const_spec((RET_HEADS, RET_CHUNK, RET_DK)),
                  _const_spec((RET_HEADS, RET_CHUNK, RET_DK)), _const_spec((RET_HEADS, 8, RET_DK)),
                  _const_spec((1, BRANCH))],
        out_specs=pl.BlockSpec((1, SEQ, BRANCH), lambda i: (i, 0, 0)),
        out_shape=jax.ShapeDtypeStruct((b, SEQ, BRANCH), BF16),
        scratch_shapes=[pltpu.VMEM((SEQ // RET_CHUNK, RET_DK, RET_DK), BF16)],
        compiler_params=pltpu.CompilerParams(dimension_semantics=("arbitrary",), vmem_limit_bytes=VMEM_LIMIT),
    )(rq3, rg3, intra, qd, kd, cd, gn)


def _merge_kernel(x_ref, g_ref, wg_ref, bg_ref, ya_ref, oc_ref, acc_ref, ng_ref, yc_ref, yd_ref, wb_ref, wo_ref, o_ref):
    T = MERGE_TILE
    TQ = ATT_TQ
    x = x_ref[...]
    h = _rms(x, g_ref[...]).astype(BF16)

    lane = lax.broadcasted_iota(jnp.int32, (TQ, 128), 1)

    def branch_out(br):
        rows = []
        for part in range(T // TQ):
            pairs = []
            for g in range(NSA_GROUPS):
                halves = []
                for j in range(NSA_HPG):
                    acc = acc_ref[part, 0, br * NSA_GROUPS + g, j * TQ:(j + 1) * TQ, :]
                    rolled = pltpu.roll(acc, NSA_DH, 1)
                    halves.append(acc * (1.0 / rolled) if j % 2 == 0 else rolled * (1.0 / acc))
                pairs += [jnp.where(lane < NSA_DH, halves[2 * c], halves[2 * c + 1]) for c in range(NSA_HPG // 2)]
            rows.append(jnp.concatenate(pairs, axis=1))
        return jnp.concatenate(rows, axis=0)

    def head_gates(br):
        return jnp.concatenate([jnp.broadcast_to(ng[:, 3 * hd + br:3 * hd + br + 1], (T, NSA_DH))
                                for hd in range(NSA_HEADS)], axis=1)

    W = 256

    def stage_dots(n, y, c):
        cols = slice(n * D_MODEL + c, n * D_MODEL + c + W)
        return _dot(h, wg_ref[:, cols]), _dot(y, wb_ref[n, :, c:c + W]), bg_ref[:, cols]

    merged = [None] * (D_MODEL // W)

    def stage_gate(c, dots):
        logits, proj, bias = dots
        term = proj * jax.nn.sigmoid(logits + bias)
        merged[c // W] = term if merged[c // W] is None else merged[c // W] + term

    pending = None

    def run_branch(n, y):
        nonlocal pending
        for c in range(0, D_MODEL, W):
            dots = stage_dots(n, y, c)
            if pending is not None:
                stage_gate(*pending)
            pending = (c, dots)

    ng = ng_ref[...]
    run_branch(0, ya_ref[...])
    yb = head_gates(0) * oc_ref[...] + head_gates(1) * branch_out(0)
    run_branch(2, yc_ref[...])
    yb = yb + head_gates(2) * branch_out(1)
    run_branch(3, yd_ref[...])
    run_branch(1, yb.astype(BF16))
    stage_gate(*pending)
    o_ref[...] = x + _dot(jnp.concatenate(merged, axis=1).astype(BF16), wo_ref[...])


def _merge(x2, g, wg, layer, bg, ya, oc, acc, ng, yc, yd, wb, wo):
    n = x2.shape[0]
    T = MERGE_TILE
    tok = lambda width: pl.BlockSpec((T, width), lambda i: (i, 0))
    return pl.pallas_call(
        _merge_kernel,
        name="merge",
        grid=(n // T,),
        in_specs=[tok(D_MODEL), _const_spec((1, D_MODEL)), _layer_spec(layer, (D_MODEL, N_BRANCH * D_MODEL)),
                  _const_spec((1, N_BRANCH * D_MODEL)), tok(BRANCH), tok(BRANCH),
                  pl.BlockSpec((T // ATT_TQ, 1) + acc.shape[2:], lambda i: (i, 0, 0, 0, 0)), tok(NG_PAD),
                  tok(BRANCH), tok(BRANCH),
                  _layer_spec(layer, (N_BRANCH, BRANCH, D_MODEL)), _layer_spec(layer, (D_MODEL, D_MODEL))],
        out_specs=tok(D_MODEL),
        out_shape=jax.ShapeDtypeStruct((n, D_MODEL), F32),
        compiler_params=pltpu.CompilerParams(dimension_semantics=("arbitrary",), vmem_limit_bytes=VMEM_LIMIT),
    )(x2, g, wg, bg, ya, oc, acc, ng, yc, yd, wb, wo)


FF_CHUNK = 256


def _mlp_kernel(x_ref, g_ref, wup_ref, cw_ref, wdn_ref, fg_ref, o_ref, ug_buf, act_buf, *, final_norm):
    T = MLP_TILE
    i = pl.program_id(0)

    @pl.when(i % (SEQ // T) == 0)
    def _():
        ug_buf[0:CONV_HALO, :] = jnp.zeros((CONV_HALO, D_FF), F32)

    x = x_ref[...]
    h = _rms(x, g_ref[...]).astype(BF16)
    for c in range(D_FF // FF_CHUNK):
        cols = slice(c * FF_CHUNK, (c + 1) * FF_CHUNK)
        ug_buf[CONV_HALO:CONV_HALO + T, cols] = _dot(h, wup_ref[:, cols])
        conv = cw_ref[CONV_TAPS - 1:CONV_TAPS, cols] * ug_buf[CONV_HALO:CONV_HALO + T, cols]
        for k in range(1, CONV_TAPS):
            conv = conv + cw_ref[CONV_TAPS - 1 - k:CONV_TAPS - k, cols] * ug_buf[CONV_HALO - k:CONV_HALO - k + T, cols]
        val = _dot(h, wup_ref[:, D_FF + c * FF_CHUNK:D_FF + (c + 1) * FF_CHUNK])
        act_buf[:, cols] = (jax.nn.silu(conv) * val).astype(BF16)
    ug_buf[0:CONV_HALO, :] = ug_buf[T:T + CONV_HALO, :]
    y = x + _dot(act_buf[...], wdn_ref[...])
    if final_norm:
        y = _rms(y, fg_ref[...])
    o_ref[...] = y


def _mlp(x2, g, wup, layer, cw, wdn, fg, final_norm):
    n = x2.shape[0]
    T = MLP_TILE
    tok = pl.BlockSpec((T, D_MODEL), lambda i: (i, 0))
    return pl.pallas_call(
        functools.partial(_mlp_kernel, final_norm=final_norm),
        name="mlp",
        grid=(n // T,),
        in_specs=[tok, _const_spec((1, D_MODEL)), _layer_spec(layer, (D_MODEL, 2 * D_FF)), _const_spec((CONV_TAPS, D_FF)),
                  _layer_spec(layer, (D_FF, D_MODEL)), _const_spec((1, D_MODEL))],
        out_specs=tok,
        out_shape=jax.ShapeDtypeStruct((n, D_MODEL), F32),
        scratch_shapes=[pltpu.VMEM((CONV_HALO + T, D_FF), F32), pltpu.VMEM((T, D_FF), BF16)],
        compiler_params=pltpu.CompilerParams(dimension_semantics=("arbitrary",), vmem_limit_bytes=VMEM_LIMIT),
    )(x2, g, wup, cw, wdn, fg)


def _rope_angles(pos, half, theta):
    inv_freq = np.power(np.float32(theta), -np.arange(half, dtype=np.float32) / np.float32(half))
    ang = pos.astype(F32)[:, None] * jnp.asarray(inv_freq)[None, :]
    return jnp.cos(ang), jnp.sin(ang)


def _nsa_rope_tables(pos):
    half = ROPE_DIM // 2
    cos, sin = _rope_angles(pos, half, ROPE_THETA)
    n = pos.shape[0]
    ones = jnp.ones((n, NSA_DH - ROPE_DIM), F32)
    zeros = jnp.zeros((n, NSA_DH - ROPE_DIM), F32)
    zh = jnp.zeros((n, half), F32)
    c = jnp.concatenate([cos, cos, ones], axis=1)
    sa = jnp.concatenate([-sin, zh, zeros], axis=1)
    sb = jnp.concatenate([zh, sin, zeros], axis=1)
    return tuple(jnp.tile(t, (1, 128 // NSA_DH)) for t in (c, sa, sb))


def _ret_rope_tables(pos):
    half = RET_DK // 2
    cos, sin = _rope_angles(pos, half, RET_THETA)
    return jnp.concatenate([cos, cos], axis=1), jnp.concatenate([-sin, sin], axis=1)


def _retention_decay_tables():
    H, C = RET_HEADS, RET_CHUNK
    log_g = np.log1p(-np.exp2(-5.0 - np.arange(H))).astype(np.float32)
    n = np.arange(C, dtype=np.float32)
    diff = n[:, None] - n[None, :]
    intra = np.where(diff >= 0, np.exp(np.maximum(diff, 0.0)[None] * log_g[:, None, None]), 0.0).astype(np.float32)
    q_dec = np.exp((n[None, :] + 1.0) * log_g[:, None]).astype(np.float32)
    k_dec = np.exp((C - 1.0 - n[None, :]) * log_g[:, None]).astype(np.float32)
    c_dec = np.exp(C * log_g).astype(np.float32)
    qd = np.broadcast_to(q_dec[:, :, None], (H, C, RET_DK))
    kd = np.broadcast_to(k_dec[:, :, None], (H, C, RET_DK))
    cd = np.broadcast_to(c_dec[:, None, None], (H, 8, RET_DK))
    return tuple(jnp.asarray(np.ascontiguousarray(t)) for t in (intra, qd, kd, cd))


def _overlap_t():
    cs0 = np.arange(N_CMP_PAD) * CMP_STRIDE
    ss0 = np.arange(N_SEL) * SEL_BLOCK
    ov = np.clip(np.minimum(cs0[:, None] + CMP_BLOCK, ss0[None, :] + SEL_BLOCK)
                 - np.maximum(cs0[:, None], ss0[None, :]), 0, None)
    ov = (ov / CMP_BLOCK).astype(np.float32)
    ov[N_CMP:, :] = 0.0
    return jnp.asarray(np.ascontiguousarray(ov.T))


def _block_onehot():
    e = np.zeros((SEQ, NSA_DH), np.float32)
    e[np.arange(SEQ), np.arange(SEQ) // SEL_BLOCK] = 1.0
    return jnp.asarray(e, dtype=BF16)


def _layer(x2, batch, layer, p, wts, tabs, final_g, final_norm):
    n = x2.shape[0]
    ya, q, kc_in, vc_in, kv2, ng, rq, rg, yd = _inproj(
        x2, p["norm1_g"][None, :], wts["main_a"], wts["main_ng"], wts["main_b"], layer, p["sc_conv"], tabs["tok"], tabs["onehot"], wts["pool"],
        p["pool_scale"][None, :])

    w1 = jnp.stack([p["cmp_w1_k"], p["cmp_w1_v"]])
    wcat = jnp.concatenate([w1[:, :CMP_STRIDE], w1[:, CMP_STRIDE:]], axis=3)
    zero = jnp.zeros_like(wcat)
    w1blk = jnp.concatenate([jnp.concatenate([wcat, zero], axis=3), jnp.concatenate([zero, wcat], axis=3)],
                            axis=2).astype(BF16)
    w1f = w1.reshape(2, CMP_BLOCK * NSA_DH, NSA_DH)
    w1f = jnp.concatenate([w1f, jnp.zeros_like(w1f)], axis=2).astype(BF16)
    pe8 = jnp.broadcast_to(p["cmp_pe"].reshape(1, -1), (8, CMP_BLOCK * NSA_DH)).astype(BF16)
    w2 = jnp.stack([p["cmp_w2_k"], p["cmp_w2_v"]])
    w2p = jnp.pad(w2, ((0, 0), (0, 128 - NSA_DH), (0, 128 - NSA_DH))).astype(BF16)
    q3 = q.reshape(batch, SEQ, BRANCH)
    ocmp, selbias = _cmp_attn(q3, kc_in.reshape(batch, SEQ, 128), vc_in.reshape(batch, SEQ, 128), pe8, w1blk, w1f,
                              w2p, tabs["cmp"], tabs["ovt"])
    acc = _sel_win(q3, kv2.reshape(batch, SEQ, KV_LANES), selbias)

    yc = _retention(rq.reshape(batch, SEQ, 3 * BRANCH), rg.reshape(batch, SEQ, BRANCH), *tabs["ret"],
                    p["ret_gn_g"][None, :])

    x2 = _merge(x2, p["norm1_g"][None, :], wts["gate"], layer, p["b_gate"][None, :], ya, ocmp.reshape(n, BRANCH),
                acc.reshape((n // ATT_TQ, 1) + acc.shape[2:]), ng, yc.reshape(n, BRANCH), yd,
                wts["branch"], wts["o"])
    return _mlp(x2, p["norm2_g"][None, :], wts["up"], layer, p["ffn_conv"], wts["down"], final_g[None, :], final_norm)


def kernel(x, norm1_g, w_in, b_gate, sc_conv, cmp_pe, cmp_w1_k, cmp_w2_k, cmp_w1_v, cmp_w2_v, ret_gn_g, pool_w, pool_scale, w_branch, w_o, norm2_g, w_up, ffn_conv, w_down, final_norm_g):
    batch, seq, d = x.shape
    assert seq == SEQ and d == D_MODEL
    depth = w_in.shape[0]
    small = dict(norm1_g=norm1_g, b_gate=b_gate, sc_conv=sc_conv, cmp_pe=cmp_pe, cmp_w1_k=cmp_w1_k,
                 cmp_w2_k=cmp_w2_k, cmp_w1_v=cmp_w1_v, cmp_w2_v=cmp_w2_v, ret_gn_g=ret_gn_g, pool_scale=pool_scale,
                 norm2_g=norm2_g, ffn_conv=ffn_conv)
    n_main = NG_OFF + NG_COLS
    w_in16 = w_in.astype(BF16)
    wts = dict(
        main_a=w_in16[:, :, :NG_OFF],
        main_ng=jnp.pad(w_in16[:, :, NG_OFF:n_main], ((0, 0), (0, 0), (0, NG_PAD - NG_COLS))),
        main_b=w_in16[:, :, n_main:GATE_OFF_ORIG],
        gate=w_in16[:, :, GATE_OFF_ORIG:], pool=pool_w.astype(BF16), branch=w_branch.astype(BF16),
        o=w_o.astype(BF16), up=w_up.astype(BF16), down=w_down.astype(BF16))
    pos = jnp.arange(SEQ)
    cmp_end = jnp.asarray(np.arange(N_CMP_PAD) * CMP_STRIDE + CMP_BLOCK - 1)
    tabs = dict(tok=_nsa_rope_tables(pos) + _ret_rope_tables(pos), cmp=_nsa_rope_tables(cmp_end),
                ret=_retention_decay_tables(), ovt=_overlap_t(), onehot=_block_onehot())
    x2 = x.reshape(batch * seq, d)
    for l in range(depth):
        p = {k: v[l] for k, v in small.items()}
        x2 = _layer(x2, batch, l, p, wts, tabs, final_norm_g, final_norm=(l == depth - 1))
    return x2.reshape(batch, seq, d)
```

```python
import functools

import numpy as np
import jax
import jax.numpy as jnp
from jax import lax
from jax.experimental import pallas as pl
from jax.experimental.pallas import tpu as pltpu

F32 = jnp.float32
BF16 = jnp.bfloat16

D_MODEL = 1024
SEQ = 2048
BRANCH = 512
N_BRANCH = 4
CONV_TAPS = 3
NSA_HEADS = 8
NSA_GROUPS = 2
NSA_HPG = NSA_HEADS // NSA_GROUPS
NSA_DH = BRANCH // NSA_HEADS
CMP_BLOCK = 32
CMP_STRIDE = 16
N_CMP = (SEQ - CMP_BLOCK) // CMP_STRIDE + 1
N_CMP_PAD = SEQ // CMP_STRIDE
SEL_BLOCK = 64
N_SEL = SEQ // SEL_BLOCK
SEL_TOP_N = 16
WINDOW = 512
ROPE_THETA = 500000.0
ROPE_DIM = NSA_DH // 4
FORCE_SCORE = 1.0e4
RET_HEADS = 4
RET_DK = BRANCH // RET_HEADS
RET_CHUNK = 128
RET_THETA = 10000.0
POOL_WINDOWS = (2, 4, 8, 16)
POOL_GROUP_DIM = BRANCH // len(POOL_WINDOWS)
D_FF = 2816
EPS = 1e-6
NEG_INF = -1e30
LOG2_E = 1.4426950408889634
Q_SCALE = NSA_DH ** -0.5 * LOG2_E

SC_OFF = 0
Q_OFF = 3 * BRANCH
KV_OFF = Q_OFF + BRANCH
NG_OFF = KV_OFF + 3 * 2 * NSA_GROUPS * NSA_DH
NG_COLS = 3 * NSA_HEADS
NG_PAD = 128
RET_OFF = NG_OFF + NG_PAD
POOL_OFF = RET_OFF + 4 * BRANCH
MAIN_COLS = POOL_OFF + BRANCH
GATE_OFF_ORIG = NG_OFF + NG_COLS + 4 * BRANCH + BRANCH

IN_TILE = 1024
MERGE_TILE = 512
MLP_TILE = 1024
CONV_HALO = 8
POOL_HALO = 16
ATT_TQ = 256
ATT_TK = 256
KV_LANES = 4 * NSA_GROUPS * 128
CMP_TQ = 2048
VMEM_LIMIT = 56 * 1024 * 1024


def _const_spec(shape):
    n = len(shape)
    return pl.BlockSpec(shape, lambda *_: (0,) * n, pipeline_mode=pl.Buffered(1))


def _layer_spec(layer, shape):
    n = len(shape)
    return pl.BlockSpec((None,) + tuple(shape), lambda *_: (layer,) + (0,) * n, pipeline_mode=pl.Buffered(1))


def _rms(x, g):
    ms = jnp.mean(x * x, axis=-1, keepdims=True)
    return x * lax.rsqrt(ms + EPS) * g


def _dot(a, b):
    return jnp.dot(a, b, preferred_element_type=F32)


def _dot_nt(a, b):
    return lax.dot_general(a, b, (((1,), (1,)), ((), ())), preferred_element_type=F32)


def _inproj_kernel(x_ref, g_ref, wa_ref, wn_ref, wb_ref, scw_ref, nc_ref, nsa_ref, nsb_ref, rc_ref, rs_ref, oh_ref, pw_ref, ps_ref,
                   ya_ref, q_ref, kc_ref, vc_ref, kv2_ref, ng_ref, rq_ref, rg_ref, yd_ref, ch_buf, u_buf):
    T = IN_TILE
    tiles_per_seq = SEQ // T
    i = pl.program_id(0)

    @pl.when(i % tiles_per_seq == 0)
    def _():
        ch_buf[0:CONV_HALO, :] = jnp.zeros((CONV_HALO, BRANCH), F32)
        u_buf[0:POOL_HALO, :] = jnp.zeros((POOL_HALO, BRANCH), F32)

    h = _rms(x_ref[...], g_ref[...]).astype(BF16)

    def proj(off, width):
        if off < NG_OFF:
            return _dot(h, wa_ref[:, off:off + width])
        if off == NG_OFF:
            return _dot(h, wn_ref[...])
        return _dot(h, wb_ref[:, off - RET_OFF:off - RET_OFF + width])

    nc, nsa, nsb = nc_ref[...], nsa_ref[...], nsb_ref[...]
    rc, rs = rc_ref[...], rs_ref[...]
    W = 256
    STAGE_LOOKAHEAD = 2

    def rope_nsa(x):
        return x * nc + pltpu.roll(x, 128 - ROPE_DIM // 2, 1) * nsa + pltpu.roll(x, ROPE_DIM // 2, 1) * nsb

    def rope_ret(x):
        return x * rc + pltpu.roll(x, RET_DK // 2, 1) * rs

    def short_conv(c0, zs):
        zb, zc, zh = zs
        cols = slice(c0, c0 + W)
        ch_buf[CONV_HALO:CONV_HALO + T, cols] = zc * zh
        conv = scw_ref[CONV_TAPS - 1:CONV_TAPS, cols] * ch_buf[CONV_HALO:CONV_HALO + T, cols]
        for k in range(1, CONV_TAPS):
            conv = conv + scw_ref[CONV_TAPS - 1 - k:CONV_TAPS - k, cols] * ch_buf[CONV_HALO - k:CONV_HALO - k + T, cols]
        ya_ref[:, cols] = (zb * conv).astype(BF16)
        ch_buf[0:CONV_HALO, cols] = ch_buf[T:T + CONV_HALO, cols]

    def pooling(c0, zs):
        (zu,) = zs
        u_buf[POOL_HALO:POOL_HALO + T, c0:c0 + W] = zu
        pos = (i % tiles_per_seq) * T + lax.broadcasted_iota(jnp.int32, (T, 1), 0)
        for gi in range(c0 // POOL_GROUP_DIM, (c0 + W) // POOL_GROUP_DIM):
            win = POOL_WINDOWS[gi]
            g0, g1 = gi * POOL_GROUP_DIM, (gi + 1) * POOL_GROUP_DIM
            s = u_buf[:, g0:g1]
            shift = 1
            while shift < win:
                s = s + pltpu.roll(s, shift, 0)
                shift *= 2
            cnt = jnp.minimum(pos + 1, win).astype(F32)
            pooled = s[POOL_HALO:, :] / cnt - zu[:, g0 - c0:g1 - c0]
            y = _dot(pooled.astype(BF16), pw_ref[gi]) * ps_ref[:, g0:g1]
            yd_ref[:, g0:g1] = y.astype(BF16)
        u_buf[0:POOL_HALO, c0:c0 + W] = u_buf[T:T + POOL_HALO, c0:c0 + W]

    def attn_q(c0, zs):
        for c in range(W // 128):
            lanes = slice(c0 + c * 128, c0 + (c + 1) * 128)
            q_ref[:, lanes] = (rope_nsa(zs[0][:, c * 128:(c + 1) * 128]) * Q_SCALE).astype(BF16)

    def cmp_kv(c0, zs):
        kc_ref[...] = zs[0][:, 0:128]
        vc_ref[...] = zs[0][:, 128:256]

    def attn_kv(c0, zs):
        branch = c0 // W - 1
        key_fill = oh_ref[...] if branch == 0 else jnp.zeros((T, NSA_DH), BF16)
        pieces = ((rope_nsa(zs[0][:, 0:128]).astype(BF16), key_fill),
                  (zs[0][:, 128:256].astype(BF16), jnp.ones((T, NSA_DH), BF16)))
        for c, (piece, fill) in enumerate(pieces):
            for g in range(NSA_GROUPS):
                lane0 = (2 * (2 * branch + c) + g) * 128
                kv2_ref[:, lane0:lane0 + 128] = jnp.concatenate([piece[:, g * NSA_DH:(g + 1) * NSA_DH], fill], axis=1)

    def attn_gates(c0, zs):
        ng_ref[...] = jax.nn.sigmoid(zs[0])

    def ret_q(c0, zs):
        for c in range(W // 128):
            rq_ref[:, c0 + c * 128:c0 + (c + 1) * 128] = rope_ret(zs[0][:, c * 128:(c + 1) * 128]).astype(BF16)

    def ret_k(c0, zs):
        for c in range(W // 128):
            rq_ref[:, BRANCH + c0 + c * 128:BRANCH + c0 + (c + 1) * 128] = (
                rope_ret(zs[0][:, c * 128:(c + 1) * 128]) * (RET_DK ** -0.5)).astype(BF16)

    def ret_v(c0, zs):
        rq_ref[:, 2 * BRANCH + c0:2 * BRANCH + c0 + W] = zs[0].astype(BF16)

    def ret_gate(c0, zs):
        rg_ref[:, c0:c0 + W] = zs[0]

    stages = []
    for c0 in range(0, BRANCH, W):
        stages.append(((SC_OFF + c0, SC_OFF + BRANCH + c0, SC_OFF + 2 * BRANCH + c0), W, c0, short_conv))
    for c0 in range(0, BRANCH, W):
        stages.append(((POOL_OFF + c0,), W, c0, pooling))
    for c0 in range(0, BRANCH, W):
        stages.append(((Q_OFF + c0,), W, c0, attn_q))
    stages.append(((KV_OFF,), W, 0, cmp_kv))
    stages.append(((KV_OFF + W,), W, W, attn_kv))
    stages.append(((KV_OFF + 2 * W,), W, 2 * W, attn_kv))
    stages.append(((NG_OFF,), NG_PAD, 0, attn_gates))
    for consumer, base in ((ret_q, RET_OFF), (ret_k, RET_OFF + BRANCH), (ret_v, RET_OFF + 2 * BRANCH),
                           (ret_gate, RET_OFF + 3 * BRANCH)):
        for c0 in range(0, BRANCH, W):
            stages.append(((base + c0,), W, c0, consumer))
    queue = []
    for offs, width, c0, consumer in stages:
        queue.append((consumer, c0, tuple(proj(off, width) for off in offs)))
        if len(queue) > STAGE_LOOKAHEAD:
            ready, ready_c0, ready_zs = queue.pop(0)
            ready(ready_c0, ready_zs)
    for ready, ready_c0, ready_zs in queue:
        ready(ready_c0, ready_zs)


def _inproj(x2, g, w_a, w_ng, w_b, layer, scw, tabs, onehot, pool_w, pool_scale):
    n = x2.shape[0]
    T = IN_TILE
    tok = lambda width: pl.BlockSpec((T, width), lambda i: (i, 0))
    tab = pl.BlockSpec((T, 128), lambda i: (i % (SEQ // T), 0))
    out_shapes = (
        jax.ShapeDtypeStruct((n, BRANCH), BF16),
        jax.ShapeDtypeStruct((n, BRANCH), BF16),
        jax.ShapeDtypeStruct((n, 128), F32),
        jax.ShapeDtypeStruct((n, 128), F32),
        jax.ShapeDtypeStruct((n, KV_LANES), BF16),
        jax.ShapeDtypeStruct((n, NG_PAD), F32),
        jax.ShapeDtypeStruct((n, 3 * BRANCH), BF16),
        jax.ShapeDtypeStruct((n, BRANCH), F32),
        jax.ShapeDtypeStruct((n, BRANCH), BF16),
    )
    return pl.pallas_call(
        _inproj_kernel,
        name="inproj",
        grid=(n // T,),
        in_specs=[tok(D_MODEL), _const_spec((1, D_MODEL)), _layer_spec(layer, (D_MODEL, NG_OFF)),
                  _layer_spec(layer, (D_MODEL, NG_PAD)), _layer_spec(layer, (D_MODEL, MAIN_COLS - RET_OFF)),
                  _const_spec((CONV_TAPS, BRANCH)), tab, tab, tab, tab, tab,
                  pl.BlockSpec((T, NSA_DH), lambda i: (i % (SEQ // T), 0)),
                  _layer_spec(layer, (len(POOL_WINDOWS), POOL_GROUP_DIM, POOL_GROUP_DIM)), _const_spec((1, BRANCH))],
        out_specs=[tok(BRANCH), tok(BRANCH), tok(128), tok(128), tok(KV_LANES), tok(NG_PAD), tok(3 * BRANCH), tok(BRANCH),
                   tok(BRANCH)],
        out_shape=out_shapes,
        scratch_shapes=[pltpu.VMEM((CONV_HALO + T, BRANCH), F32), pltpu.VMEM((POOL_HALO + T, BRANCH), F32)],
        compiler_params=pltpu.CompilerParams(dimension_semantics=("arbitrary",), vmem_limit_bytes=VMEM_LIMIT),
    )(x2, g, w_a, w_ng, w_b, scw, *tabs, onehot, pool_w, pool_scale)


def _compress_blocks(kc_ref, vc_ref, pe_ref, w1_ref, w1f_ref, w2_ref, cc_ref, csa_ref, csb_ref, store):
    for kv, x_ref in enumerate((kc_ref, vc_ref)):
        acc = jnp.zeros((N_CMP_PAD, 4 * NSA_DH), F32)
        for l in range(CMP_STRIDE):
            x = x_ref[0, pl.ds(l, N_CMP_PAD, stride=CMP_STRIDE), :].astype(BF16)
            acc = acc + _dot(x, w1_ref[kv, l])
        bias = _dot(pe_ref[...], w1f_ref[kv])[0:1, :]
        for g in range(NSA_GROUPS):
            a = acc[:, g * 128:(g + 1) * 128]
            hid = jax.nn.gelu(a + pltpu.roll(pltpu.roll(a, N_CMP_PAD - 1, 0), NSA_DH, 1) + bias)
            y = _dot(hid.astype(BF16), w2_ref[kv])
            if kv == 0:
                y = (y * cc_ref[...] + pltpu.roll(y, 128 - ROPE_DIM // 2, 1) * csa_ref[...]
                     + pltpu.roll(y, ROPE_DIM // 2, 1) * csb_ref[...])
            store(kv * NSA_GROUPS + g, y[:, 0:NSA_DH].astype(BF16))


def _cmp_attn_kernel(q_ref, kin_ref, vin_ref, pe_ref, w1_ref, w1f_ref, w2_ref, cc_ref, csa_ref, csb_ref, ovt_ref,
                     o_ref, sb_ref, kcvc_ref):
    R = CMP_TQ
    g = pl.program_id(1)
    r = pl.program_id(2)

    @pl.when((g == 0) & (r == 0))
    def _():
        def store(idx, y):
            kcvc_ref[idx] = y
        _compress_blocks(kin_ref, vin_ref, pe_ref, w1_ref, w1f_ref, w2_ref, cc_ref, csa_ref, csb_ref, store)

    kc, vc = kcvc_ref[g], kcvc_ref[NSA_GROUPS + g]
    t_q = r * R + lax.broadcasted_iota(jnp.int32, (N_CMP_PAD, R), 1)
    blk_end = lax.broadcasted_iota(jnp.int32, (N_CMP_PAD, R), 0) * CMP_STRIDE + (CMP_BLOCK - 1)
    mask_bias = jnp.where(blk_end <= t_q, 0.0, NEG_INF)
    any_block = (r * R + lax.broadcasted_iota(jnp.int32, (1, R), 1)) >= CMP_BLOCK - 1
    psum = jnp.zeros((N_CMP_PAD, R), F32)
    for j in range(NSA_HPG):
        qj = q_ref[0, :, j * NSA_DH:(j + 1) * NSA_DH]
        sm = _dot_nt(kc, qj) + mask_bias
        e = jnp.exp2(sm - jnp.max(sm, axis=0, keepdims=True))
        p = e * jnp.where(any_block, 1.0 / jnp.sum(e, axis=0, keepdims=True), 0.0)
        o_ref[0, :, j * NSA_DH:(j + 1) * NSA_DH] = lax.dot_general(
            p.astype(BF16), vc, (((0,), (0,)), ((), ())), preferred_element_type=F32)
        psum = psum + p
    imp = jnp.dot(ovt_ref[...], psum, preferred_element_type=F32, precision=lax.Precision.HIGHEST)
    cur = (r * R + lax.broadcasted_iota(jnp.int32, (N_SEL, R), 1)) // SEL_BLOCK
    bid = lax.broadcasted_iota(jnp.int32, (N_SEL, R), 0)
    forced = (bid == 0) | (bid == cur) | (bid == cur - 1)
    imp = jnp.where(forced, FORCE_SCORE, jnp.where(bid > cur, -FORCE_SCORE, imp))
    rank = jnp.zeros((N_SEL, R), F32)
    for m in range(N_SEL):
        a = imp[m:m + 1, :]
        before = jnp.where(bid > m, jnp.where(a >= imp, 1.0, 0.0), jnp.where(a > imp, 1.0, 0.0))
        rank = rank + before
    bias_t = jnp.where(rank < SEL_TOP_N, 0.0, NEG_INF)
    bias = bias_t.T
    sb_ref[0, 0] = jnp.concatenate([bias, jnp.zeros((R, NSA_DH - N_SEL), F32)], axis=1).astype(BF16)


def _cmp_attn(q3, kc3, vc3, pe8, w1blk, w1f, w2p, ctabs, ovt):
    b = q3.shape[0]
    R = CMP_TQ
    gw = NSA_HPG * NSA_DH
    kvspec = pl.BlockSpec((1, SEQ, 128), lambda i, g, r: (i, 0, 0))
    return pl.pallas_call(
        _cmp_attn_kernel,
        name="cmp_attn",
        grid=(b, NSA_GROUPS, SEQ // R),
        in_specs=[pl.BlockSpec((1, R, gw), lambda i, g, r: (i, r, g)), kvspec, kvspec,
                  _const_spec((8, CMP_BLOCK * NSA_DH)), _const_spec((2, CMP_STRIDE, 128, 4 * NSA_DH)),
                  _const_spec((2, CMP_BLOCK * NSA_DH, 128)), _const_spec((2, 128, 128)),
                  _const_spec((N_CMP_PAD, 128)), _const_spec((N_CMP_PAD, 128)), _const_spec((N_CMP_PAD, 128)),
                  _const_spec((N_SEL, N_CMP_PAD))],
        out_specs=[pl.BlockSpec((1, R, gw), lambda i, g, r: (i, r, g)),
                   pl.BlockSpec((1, 1, R, NSA_DH), lambda i, g, r: (i, g, r, 0))],
        out_shape=(jax.ShapeDtypeStruct((b, SEQ, BRANCH), F32),
                   jax.ShapeDtypeStruct((b, NSA_GROUPS, SEQ, NSA_DH), BF16)),
        scratch_shapes=[pltpu.VMEM((2 * NSA_GROUPS, N_CMP_PAD, NSA_DH), BF16)],
        compiler_params=pltpu.CompilerParams(dimension_semantics=("arbitrary",) * 3, vmem_limit_bytes=VMEM_LIMIT),
    )(q3, kc3, vc3, pe8, w1blk, w1f, w2p, *ctabs, ovt)


def _sel_win_kernel(q_ref, kv_ref, sb_ref, acc_ref, qa_ref, m_ref):
    TQ, TK = ATT_TQ, ATT_TK
    M = NSA_HPG * TQ
    qi = pl.program_id(1)

    def chunk(rows, c, g):
        return kv_ref[0, rows, (2 * c + g) * 128:(2 * c + g + 1) * 128]

    for g in range(NSA_GROUPS):
        sb = sb_ref[0, g]
        for j in range(NSA_HPG):
            h = g * NSA_HPG + j
            qa_ref[g, j * TQ:(j + 1) * TQ, :] = jnp.concatenate([q_ref[0, :, h * NSA_DH:(h + 1) * NSA_DH], sb], axis=1)

    def online(idx, s, v, first=False):
        row_max = jnp.max(s, axis=-1, keepdims=True)
        if first:
            m_new = jnp.broadcast_to(row_max, (M, 128))
        else:
            m = m_ref[idx]
            m_new = jnp.maximum(m, row_max)
        p = jnp.exp2(s - pltpu.repeat(m_new, s.shape[1] // 128, axis=1))
        pv = _dot(p.astype(BF16), v)
        acc_ref[0, 0, idx] = pv if first else jnp.exp2(m - m_new) * acc_ref[0, 0, idx] + pv
        m_ref[idx] = m_new

    def far_tiles(k0, width):
        rows = pl.ds(pl.multiple_of(k0, TK), width)
        for g in range(NSA_GROUPS):
            online(g, _dot_nt(qa_ref[g], chunk(rows, 0, g)), chunk(rows, 1, g))

    def near_tile(t, first):
        rows = pl.ds(pl.multiple_of(t * TK, TK), TK)
        dist = (lax.broadcasted_iota(jnp.int32, (TQ, TK), 0) - lax.broadcasted_iota(jnp.int32, (TQ, TK), 1)
                + (qi - t) * TK)
        causal_bias = jnp.where(dist >= 0, 0.0, NEG_INF)
        window_bias = jnp.where(lax.bitcast_convert_type(dist, jnp.uint32) < WINDOW, 0.0, NEG_INF)

        def masked(s, bias):
            return (s.reshape(NSA_HPG, TQ, TK) + bias[None]).reshape(M, TK)

        for g in range(NSA_GROUPS):
            qa = qa_ref[g]
            online(g, masked(_dot_nt(qa, chunk(rows, 0, g)), causal_bias), chunk(rows, 1, g), first)
            online(NSA_GROUPS + g, masked(_dot_nt(qa, chunk(rows, 2, g)), window_bias), chunk(rows, 3, g), first)

    max_near = WINDOW // TK + 1
    for count in range(1, max_near + 1):
        @pl.when((qi == count - 1) if count < max_near else (qi >= count - 1))
        def _(count=count):
            for back in range(count):
                near_tile(qi - back, first=(back == 0))

    n_far = jnp.maximum(qi - WINDOW // TK, 0)

    def far_pair(i, carry):
        far_tiles(i * (2 * TK), 2 * TK)
        return carry

    lax.fori_loop(0, n_far // 2, far_pair, 0)

    @pl.when(n_far % 2 == 1)
    def _():
        far_tiles((n_far - 1) * TK, TK)


def _sel_win(q3, kv3, selbias):
    b = q3.shape[0]
    TQ = ATT_TQ
    M = NSA_HPG * TQ
    return pl.pallas_call(
        _sel_win_kernel,
        name="sel_win_attn",
        grid=(b, SEQ // TQ),
        in_specs=[pl.BlockSpec((1, TQ, BRANCH), lambda i, t: (i, t, 0)),
                  pl.BlockSpec((1, SEQ, KV_LANES), lambda i, t: (i, 0, 0)),
                  pl.BlockSpec((1, NSA_GROUPS, TQ, NSA_DH), lambda i, t: (i, 0, t, 0))],
        out_specs=pl.BlockSpec((1, 1, 2 * NSA_GROUPS, M, 128), lambda i, t: (i, t, 0, 0, 0)),
        out_shape=jax.ShapeDtypeStruct((b, SEQ // TQ, 2 * NSA_GROUPS, M, 128), F32),
        scratch_shapes=[pltpu.VMEM((NSA_GROUPS, M, 128), BF16), pltpu.VMEM((2 * NSA_GROUPS, M, 128), F32)],
        compiler_params=pltpu.CompilerParams(dimension_semantics=("arbitrary", "arbitrary"),
                                             vmem_limit_bytes=VMEM_LIMIT),
    )(q3, kv3, selbias)


def _retention_kernel(qkv_ref, zg_ref, intra_ref, qd_ref, kd_ref, cd_ref, gn_ref, y_ref, st_ref):
    C = RET_CHUNK
    NC = SEQ // C

    def bdot(a, b, ca, cb):
        return lax.dot_general(a, b, (((ca,), (cb,)), ((0,), (0,))), preferred_element_type=F32)

    for h in range(RET_HEADS):
        lanes = slice(h * RET_DK, (h + 1) * RET_DK)
        q3, k3, v3 = (qkv_ref[0, :, part * BRANCH + h * RET_DK:part * BRANCH + (h + 1) * RET_DK].reshape(NC, C, RET_DK)
                      for part in range(3))
        scores = bdot(q3, k3, 2, 2) * intra_ref[h][None]
        inner = bdot(scores.astype(BF16), v3, 2, 1)
        kdec = (k3.astype(F32) * kd_ref[h][None]).astype(BF16)
        kv = bdot(kdec, v3, 1, 1)
        cd = cd_ref[h, 0:1, :]
        state = jnp.zeros((RET_DK, RET_DK), F32)
        for c in range(NC):
            st_ref[c] = state.astype(BF16)
            state = state * cd + kv[c]
        qdec = (q3.astype(F32) * qd_ref[h][None]).astype(BF16)
        o = inner + bdot(qdec, st_ref[...], 2, 1)
        mu = jnp.mean(o, axis=-1, keepdims=True)
        d = o - mu
        var = jnp.mean(d * d, axis=-1, keepdims=True)
        o = (d * lax.rsqrt(var + EPS)).reshape(SEQ, RET_DK) * gn_ref[:, lanes]
        y_ref[0, :, lanes] = (o * jax.nn.silu(zg_ref[0, :, lanes])).astype(BF16)


def _retention(rq3, rg3, intra, qd, kd, cd, gn):
    b = rq3.shape[0]
    return pl.pallas_call(
        _retention_kernel,
        name="retention",
        grid=(b,),
        in_specs=[pl.BlockSpec((1, SEQ, 3 * BRANCH), lambda i: (i, 0, 0)),
                  pl.BlockSpec((1, SEQ, BRANCH), lambda i: (i, 0, 0)),
                  _const_spec((RET_HEADS, RET_CHUNK, RET_DK)), _const_spec((RET_HEADS, RET_CHUNK, RET_DK)),
                  _const_spec((RET_HEADS, RET_CHUNK, RET_DK)), _const_spec((RET_HEADS, 8, RET_DK)),
                  _const_spec((1, BRANCH))],
        out_specs=pl.BlockSpec((1, SEQ, BRANCH), lambda i: (i, 0, 0)),
        out_shape=jax.ShapeDtypeStruct((b, SEQ, BRANCH), BF16),
        scratch_shapes=[pltpu.VMEM((SEQ // RET_CHUNK, RET_DK, RET_DK), BF16)],
        compiler_params=pltpu.CompilerParams(dimension_semantics=("arbitrary",), vmem_limit_bytes=VMEM_LIMIT),
    )(rq3, rg3, intra, qd, kd, cd, gn)


def _merge_kernel(x_ref, g_ref, wg_ref, bg_ref, ya_ref, oc_ref, acc_ref, ng_ref, yc_ref, yd_ref, wb_ref, wo_ref, o_ref):
    T = MERGE_TILE
    TQ = ATT_TQ
    x = x_ref[...]
    h = _rms(x, g_ref[...]).astype(BF16)

    lane = lax.broadcasted_iota(jnp.int32, (TQ, 128), 1)

    def branch_out(br):
        rows = []
        for part in range(T // TQ):
            pairs = []
            for g in range(NSA_GROUPS):
                halves = []
                for j in range(NSA_HPG):
                    acc = acc_ref[part, 0, br * NSA_GROUPS + g, j * TQ:(j + 1) * TQ, :]
                    rolled = pltpu.roll(acc, NSA_DH, 1)
                    halves.append(acc * (1.0 / rolled) if j % 2 == 0 else rolled * (1.0 / acc))
                pairs += [jnp.where(lane < NSA_DH, halves[2 * c], halves[2 * c + 1]) for c in range(NSA_HPG // 2)]
            rows.append(jnp.concatenate(pairs, axis=1))
        return jnp.concatenate(rows, axis=0)

    def head_gates(br):
        return jnp.concatenate([jnp.broadcast_to(ng[:, 3 * hd + br:3 * hd + br + 1], (T, NSA_DH))
                                for hd in range(NSA_HEADS)], axis=1)

    W = 256

    def stage_dots(n, y, c):
        cols = slice(n * D_MODEL + c, n * D_MODEL + c + W)
        return _dot(h, wg_ref[:, cols]), _dot(y, wb_ref[n, :, c:c + W]), bg_ref[:, cols]

    merged = [None] * (D_MODEL // W)

    def stage_gate(c, dots):
        logits, proj, bias = dots
        term = proj * jax.nn.sigmoid(logits + bias)
        merged[c // W] = term if merged[c // W] is None else merged[c // W] + term

    pending = None

    def run_branch(n, y):
        nonlocal pending
        for c in range(0, D_MODEL, W):
            dots = stage_dots(n, y, c)
            if pending is not None:
                stage_gate(*pending)
            pending = (c, dots)

    ng = ng_ref[...]
    run_branch(0, ya_ref[...])
    yb = head_gates(0) * oc_ref[...] + head_gates(1) * branch_out(0)
    run_branch(2, yc_ref[...])
    yb = yb + head_gates(2) * branch_out(1)
    run_branch(3, yd_ref[...])
    run_branch(1, yb.astype(BF16))
    stage_gate(*pending)
    o_ref[...] = x + _dot(jnp.concatenate(merged, axis=1).astype(BF16), wo_ref[...])


def _merge(x2, g, wg, layer, bg, ya, oc, acc, ng, yc, yd, wb, wo):
    n = x2.shape[0]
    T = MERGE_TILE
    tok = lambda width: pl.BlockSpec((T, width), lambda i: (i, 0))
    return pl.pallas_call(
        _merge_kernel,
        name="merge",
        grid=(n // T,),
        in_specs=[tok(D_MODEL), _const_spec((1, D_MODEL)), _layer_spec(layer, (D_MODEL, N_BRANCH * D_MODEL)),
                  _const_spec((1, N_BRANCH * D_MODEL)), tok(BRANCH), tok(BRANCH),
                  pl.BlockSpec((T // ATT_TQ, 1) + acc.shape[2:], lambda i: (i, 0, 0, 0, 0)), tok(NG_PAD),
                  tok(BRANCH), tok(BRANCH),
                  _layer_spec(layer, (N_BRANCH, BRANCH, D_MODEL)), _layer_spec(layer, (D_MODEL, D_MODEL))],
        out_specs=tok(D_MODEL),
        out_shape=jax.ShapeDtypeStruct((n, D_MODEL), F32),
        compiler_params=pltpu.CompilerParams(dimension_semantics=("arbitrary",), vmem_limit_bytes=VMEM_LIMIT),
    )(x2, g, wg, bg, ya, oc, acc, ng, yc, yd, wb, wo)


FF_CHUNK = 256


def _mlp_kernel(x_ref, g_ref, wup_ref, cw_ref, wdn_ref, fg_ref, o_ref, ug_buf, act_buf, *, final_norm):
    T = MLP_TILE
    i = pl.program_id(0)

    @pl.when(i % (SEQ // T) == 0)
    def _():
        ug_buf[0:CONV_HALO, :] = jnp.zeros((CONV_HALO, D_FF), F32)

    x = x_ref[...]
    h = _rms(x, g_ref[...]).astype(BF16)
    for c in range(D_FF // FF_CHUNK):
        cols = slice(c * FF_CHUNK, (c + 1) * FF_CHUNK)
        ug_buf[CONV_HALO:CONV_HALO + T, cols] = _dot(h, wup_ref[:, cols])
        conv = cw_ref[CONV_TAPS - 1:CONV_TAPS, cols] * ug_buf[CONV_HALO:CONV_HALO + T, cols]
        for k in range(1, CONV_TAPS):
            conv = conv + cw_ref[CONV_TAPS - 1 - k:CONV_TAPS - k, cols] * ug_buf[CONV_HALO - k:CONV_HALO - k + T, cols]
        val = _dot(h, wup_ref[:, D_FF + c * FF_CHUNK:D_FF + (c + 1) * FF_CHUNK])
        act_buf[:, cols] = (jax.nn.silu(conv) * val).astype(BF16)
    ug_buf[0:CONV_HALO, :] = ug_buf[T:T + CONV_HALO, :]
    y = x + _dot(act_buf[...], wdn_ref[...])
    if final_norm:
        y = _rms(y, fg_ref[...])
    o_ref[...] = y


def _mlp(x2, g, wup, layer, cw, wdn, fg, final_norm):
    n = x2.shape[0]
    T = MLP_TILE
    tok = pl.BlockSpec((T, D_MODEL), lambda i: (i, 0))
    return pl.pallas_call(
        functools.partial(_mlp_kernel, final_norm=final_norm),
        name="mlp",
        grid=(n // T,),
        in_specs=[tok, _const_spec((1, D_MODEL)), _layer_spec(layer, (D_MODEL, 2 * D_FF)), _const_spec((CONV_TAPS, D_FF)),
                  _layer_spec(layer, (D_FF, D_MODEL)), _const_spec((1, D_MODEL))],
        out_specs=tok,
        out_shape=jax.ShapeDtypeStruct((n, D_MODEL), F32),
        scratch_shapes=[pltpu.VMEM((CONV_HALO + T, D_FF), F32), pltpu.VMEM((T, D_FF), BF16)],
        compiler_params=pltpu.CompilerParams(dimension_semantics=("arbitrary",), vmem_limit_bytes=VMEM_LIMIT),
    )(x2, g, wup, cw, wdn, fg)


def _rope_angles(pos, half, theta):
    inv_freq = np.power(np.float32(theta), -np.arange(half, dtype=np.float32) / np.float32(half))
    ang = pos.astype(F32)[:, None] * jnp.asarray(inv_freq)[None, :]
    return jnp.cos(ang), jnp.sin(ang)


def _nsa_rope_tables(pos):
    half = ROPE_DIM // 2
    cos, sin = _rope_angles(pos, half, ROPE_THETA)
    n = pos.shape[0]
    ones = jnp.ones((n, NSA_DH - ROPE_DIM), F32)
    zeros = jnp.zeros((n, NSA_DH - ROPE_DIM), F32)
    zh = jnp.zeros((n, half), F32)
    c = jnp.concatenate([cos, cos, ones], axis=1)
    sa = jnp.concatenate([-sin, zh, zeros], axis=1)
    sb = jnp.concatenate([zh, sin, zeros], axis=1)
    return tuple(jnp.tile(t, (1, 128 // NSA_DH)) for t in (c, sa, sb))


def _ret_rope_tables(pos):
    half = RET_DK // 2
    cos, sin = _rope_angles(pos, half, RET_THETA)
    return jnp.concatenate([cos, cos], axis=1), jnp.concatenate([-sin, sin], axis=1)


def _retention_decay_tables():
    H, C = RET_HEADS, RET_CHUNK
    log_g = np.log1p(-np.exp2(-5.0 - np.arange(H))).astype(np.float32)
    n = np.arange(C, dtype=np.float32)
    diff = n[:, None] - n[None, :]
    intra = np.where(diff >= 0, np.exp(np.maximum(diff, 0.0)[None] * log_g[:, None, None]), 0.0).astype(np.float32)
    q_dec = np.exp((n[None, :] + 1.0) * log_g[:, None]).astype(np.float32)
    k_dec = np.exp((C - 1.0 - n[None, :]) * log_g[:, None]).astype(np.float32)
    c_dec = np.exp(C * log_g).astype(np.float32)
    qd = np.broadcast_to(q_dec[:, :, None], (H, C, RET_DK))
    kd = np.broadcast_to(k_dec[:, :, None], (H, C, RET_DK))
    cd = np.broadcast_to(c_dec[:, None, None], (H, 8, RET_DK))
    return tuple(jnp.asarray(np.ascontiguousarray(t)) for t in (intra, qd, kd, cd))


def _overlap_t():
    cs0 = np.arange(N_CMP_PAD) * CMP_STRIDE
    ss0 = np.arange(N_SEL) * SEL_BLOCK
    ov = np.clip(np.minimum(cs0[:, None] + CMP_BLOCK, ss0[None, :] + SEL_BLOCK)
                 - np.maximum(cs0[:, None], ss0[None, :]), 0, None)
    ov = (ov / CMP_BLOCK).astype(np.float32)
    ov[N_CMP:, :] = 0.0
    return jnp.asarray(np.ascontiguousarray(ov.T))


def _block_onehot():
    e = np.zeros((SEQ, NSA_DH), np.float32)
    e[np.arange(SEQ), np.arange(SEQ) // SEL_BLOCK] = 1.0
    return jnp.asarray(e, dtype=BF16)


def _layer(x2, batch, layer, p, wts, tabs, final_g, final_norm):
    n = x2.shape[0]
    ya, q, kc_in, vc_in, kv2, ng, rq, rg, yd = _inproj(
        x2, p["norm1_g"][None, :], wts["main_a"], wts["main_ng"], wts["main_b"], layer, p["sc_conv"], tabs["tok"], tabs["onehot"], wts["pool"],
        p["pool_scale"][None, :])

    w1 = jnp.stack([p["cmp_w1_k"], p["cmp_w1_v"]])
    wcat = jnp.concatenate([w1[:, :CMP_STRIDE], w1[:, CMP_STRIDE:]], axis=3)
    zero = jnp.zeros_like(wcat)
    w1blk = jnp.concatenate([jnp.concatenate([wcat, zero], axis=3), jnp.concatenate([zero, wcat], axis=3)],
                            axis=2).astype(BF16)
    w1f = w1.reshape(2, CMP_BLOCK * NSA_DH, NSA_DH)
    w1f = jnp.concatenate([w1f, jnp.zeros_like(w1f)], axis=2).astype(BF16)
    pe8 = jnp.broadcast_to(p["cmp_pe"].reshape(1, -1), (8, CMP_BLOCK * NSA_DH)).astype(BF16)
    w2 = jnp.stack([p["cmp_w2_k"], p["cmp_w2_v"]])
    w2p = jnp.pad(w2, ((0, 0), (0, 128 - NSA_DH), (0, 128 - NSA_DH))).astype(BF16)
    q3 = q.reshape(batch, SEQ, BRANCH)
    ocmp, selbias = _cmp_attn(q3, kc_in.reshape(batch, SEQ, 128), vc_in.reshape(batch, SEQ, 128), pe8, w1blk, w1f,
                              w2p, tabs["cmp"], tabs["ovt"])
    acc = _sel_win(q3, kv2.reshape(batch, SEQ, KV_LANES), selbias)

    yc = _retention(rq.reshape(batch, SEQ, 3 * BRANCH), rg.reshape(batch, SEQ, BRANCH), *tabs["ret"],
                    p["ret_gn_g"][None, :])

    x2 = _merge(x2, p["norm1_g"][None, :], wts["gate"], layer, p["b_gate"][None, :], ya, ocmp.reshape(n, BRANCH),
                acc.reshape((n // ATT_TQ, 1) + acc.shape[2:]), ng, yc.reshape(n, BRANCH), yd,
                wts["branch"], wts["o"])
    return _mlp(x2, p["norm2_g"][None, :], wts["up"], layer, p["ffn_conv"], wts["down"], final_g[None, :], final_norm)


def kernel(x, norm1_g, w_in, b_gate, sc_conv, cmp_pe, cmp_w1_k, cmp_w2_k, cmp_w1_v, cmp_w2_v, ret_gn_g, pool_w, pool_scale, w_branch, w_o, norm2_g, w_up, ffn_conv, w_down, final_norm_g):
    batch, seq, d = x.shape
    assert seq == SEQ and d == D_MODEL
    depth = w_in.shape[0]
    small = dict(norm1_g=norm1_g, b_gate=b_gate, sc_conv=sc_conv, cmp_pe=cmp_pe, cmp_w1_k=cmp_w1_k,
                 cmp_w2_k=cmp_w2_k, cmp_w1_v=cmp_w1_v, cmp_w2_v=cmp_w2_v, ret_gn_g=ret_gn_g, pool_scale=pool_scale,
                 norm2_g=norm2_g, ffn_conv=ffn_conv)
    n_main = NG_OFF + NG_COLS
    w_in16 = w_in.astype(BF16)
    wts = dict(
        main_a=w_in16[:, :, :NG_OFF],
        main_ng=jnp.pad(w_in16[:, :, NG_OFF:n_main], ((0, 0), (0, 0), (0, NG_PAD - NG_COLS))),
        main_b=w_in16[:, :, n_main:GATE_OFF_ORIG],
        gate=w_in16[:, :, GATE_OFF_ORIG:], pool=pool_w.astype(BF16), branch=w_branch.astype(BF16),
        o=w_o.astype(BF16), up=w_up.astype(BF16), down=w_down.astype(BF16))
    pos = jnp.arange(SEQ)
    cmp_end = jnp.asarray(np.arange(N_CMP_PAD) * CMP_STRIDE + CMP_BLOCK - 1)
    tabs = dict(tok=_nsa_rope_tables(pos) + _ret_rope_tables(pos), cmp=_nsa_rope_tables(cmp_end),
                ret=_retention_decay_tables(), ovt=_overlap_t(), onehot=_block_onehot())
    x2 = x.reshape(batch * seq, d)
    for l in range(depth):
        p = {k: v[l] for k, v in small.items()}
        x2 = _layer(x2, batch, l, p, wts, tabs, final_norm_g, final_norm=(l == depth - 1))
    return x2.reshape(batch, seq, d)
```
